```python
import math
import functools
import jax
import jax.numpy as jnp
from jax import lax
import numpy as np

D_MODEL = 2048
BATCH = 8
SEQ = 2048
DEPTH = 2
DEC_BATCH = 32
DEC_SEQ = 1
PAST_LEN = 8192
PAGE_SIZE = 128

N_EVEN = (DEPTH + 1) // 2
N_ODD = DEPTH // 2
A_WIDTH = D_MODEL
A_GROUPS = 8
A_GROUP_DIM = A_WIDTH // A_GROUPS
A_CHUNK = 128
B_WIDTH = D_MODEL
B_HEAD_DIM = 64
B_HEADS = B_WIDTH // B_HEAD_DIM
B_GROUPS = 4
B_HEADS_PER_GROUP = B_HEADS // B_GROUPS
B_STATE = 128
B_CONV = 4
B_CONV_DIM = B_WIDTH + 2 * B_GROUPS * B_STATE
B_CHUNK = 128
C_WIDTH = D_MODEL
C_HEADS = 8
C_BLOCK = C_WIDTH // C_HEADS
C_CONV = 4
C_GATE_C = 8.0
D_HEADS = 16
D_HEAD_DIM = 128
D_WIDTH = D_HEADS * D_HEAD_DIM
D_QBLOCK = 128
D_FF = 5632
FFN_CONV = 3
EPS = 1e-6

IN_EVEN = 2 * A_WIDTH + B_WIDTH + B_CONV_DIM + B_HEADS
OUT_EVEN = A_WIDTH + B_WIDTH
IN_ODD = 2 * C_WIDTH + 3 * D_WIDTH + D_HEADS
OUT_ODD = C_WIDTH + D_WIDTH

kernel_name = 'hybrid_sgu_ssd_rglru_fox_decode_step'


def _split(t, sizes):
    cuts = [int(c) for c in np.cumsum(sizes)[:-1]]
    return jnp.split(t, cuts, axis=-1)


def rmsnorm(x, g):
    xf = x.astype(jnp.float32)
    y = xf * lax.rsqrt(jnp.mean(xf * xf, axis=-1, keepdims=True) + EPS)
    return (y * g.astype(jnp.float32)).astype(x.dtype)


def group_rmsnorm(y, g, groups):
    lead = y.shape[:-1]
    yg = y.reshape(lead + (groups, y.shape[-1] // groups))
    yg = yg * lax.rsqrt(jnp.mean(yg * yg, axis=-1, keepdims=True) + EPS)
    return yg.reshape(y.shape) * g.astype(jnp.float32)


def layernorm(x, g, b):
    xf = x.astype(jnp.float32)
    xc = xf - jnp.mean(xf, axis=-1, keepdims=True)
    var = jnp.mean(xc * xc, axis=-1, keepdims=True)
    return (xc * lax.rsqrt(var + EPS) * g.astype(jnp.float32) + b.astype(jnp.float32)).astype(x.dtype)


def causal_dwconv(x, prefix, w, b):
    width = w.shape[0]
    t = x.shape[1]
    xp = jnp.concatenate([prefix.astype(x.dtype), x], axis=1)
    y = b + xp[:, width - 1:] * w[width - 1]
    for k in range(width - 1):
        y = y + xp[:, k:k + t] * w[k]
    return y, xp[:, t:]


def chunk_sgu(u, v, w_s, b_s):
    bn, t, _ = v.shape
    nc = -(-t // A_CHUNK)
    pad = nc * A_CHUNK - t
    vc = jnp.pad(v, ((0, 0), (0, pad), (0, 0))).reshape(bn, nc, A_CHUNK, A_GROUPS, A_GROUP_DIM)
    causal = jnp.tril(jnp.ones((A_CHUNK, A_CHUNK), dtype=bool))
    w = jnp.where(causal, w_s, jnp.zeros_like(w_s))
    s = jnp.einsum('gts,bnsgc->bntgc', w, vc) + b_s.T[:, :, None]
    s = s.reshape(bn, nc * A_CHUNK, A_WIDTH)[:, :t]
    return u * s


def ssd_scan(x, dt, a, bm, cm, h0):
    bn, t = x.shape[0], x.shape[1]
    q = min(B_CHUNK, t)
    nc = -(-t // q)
    pad = nc * q - t

    def chunks(z):
        z = jnp.pad(z, [(0, 0), (0, pad)] + [(0, 0)] * (z.ndim - 2))
        return z.reshape((bn, nc, q) + z.shape[2:])

    g, e = B_GROUPS, B_HEADS_PER_GROUP
    xc = chunks(x.astype(jnp.float32)).reshape(bn, nc, q, g, e, B_HEAD_DIM)
    dtc = chunks(dt).reshape(bn, nc, q, g, e)
    bc = chunks(bm.astype(jnp.float32))
    cc = chunks(cm.astype(jnp.float32))
    acs = jnp.cumsum(dtc * a.reshape(g, e), axis=2)
    seg = acs[:, :, :, None] - acs[:, :, None, :]
    causal = jnp.tril(jnp.ones((q, q), dtype=bool))[:, :, None, None]
    decay = jnp.exp(jnp.where(causal, seg, -jnp.inf))
    cb = jnp.einsum('bctgn,bcsgn->bctsg', cc, bc)
    y_diag = jnp.einsum('bctsg,bctsge,bcsge,bcsgep->bctgep', cb, decay, dtc, xc)
    to_end = jnp.exp(acs[:, :, -1:] - acs) * dtc
    chunk_states = jnp.einsum('bcsgn,bcsge,bcsgep->bcgepn', bc, to_end, xc)
    chunk_decay = jnp.exp(acs[:, :, -1])

    def step(h, inp):
        st, dc = inp
        return dc[..., None, None] * h + st, h

    h_init = h0.astype(jnp.float32).reshape(bn, g, e, B_HEAD_DIM, B_STATE)
    h_last, h_in = lax.scan(step, h_init, (jnp.moveaxis(chunk_states, 1, 0), jnp.moveaxis(chunk_decay, 1, 0)))
    h_in = jnp.moveaxis(h_in, 0, 1)
    y_off = jnp.einsum('bctgn,bctge,bcgepn->bctgep', cc, jnp.exp(acs), h_in)
    y = (y_diag + y_off).reshape(bn, nc * q, B_HEADS, B_HEAD_DIM)[:, :t]
    return y, h_last.reshape(bn, B_HEADS, B_HEAD_DIM, B_STATE)


def linear_recurrence(a, b, h0):
    b = b.at[:, 0].add(a[:, 0] * h0)

    def comb(left, right):
        al, bl = left
        ar, br = right
        return al * ar, ar * bl + br

    _, h = lax.associative_scan(comb, (a, b), axis=1)
    return h, h[:, -1]


def fox_block(q, c_q, q_pos, k, v, c_k):
    s = jnp.einsum('qhd,khd->hqk', q, k).astype(jnp.float32) * (D_HEAD_DIM ** -0.5)
    s = s + c_q.T[:, :, None] - c_k.T[:, None, :]
    k_pos = jnp.arange(k.shape[0])
    s = jnp.where(k_pos[None, None, :] <= q_pos[None, :, None], s, -jnp.inf)
    p = jax.nn.softmax(s, axis=-1)
    return jnp.einsum('hqk,khd->qhd', p.astype(v.dtype), v)


def fox_prompt_attend(q, k, v, logf):
    bn, t = q.shape[0], q.shape[1]
    c = jnp.cumsum(logf, axis=1)
    nb = t // D_QBLOCK
    qb = jnp.moveaxis(q.reshape(bn, nb, D_QBLOCK, D_HEADS, D_HEAD_DIM), 1, 0)
    cqb = jnp.moveaxis(c.reshape(bn, nb, D_QBLOCK, D_HEADS), 1, 0)
    pos = jnp.arange(t).reshape(nb, D_QBLOCK)
    attend_b = jax.vmap(fox_block, in_axes=(0, 0, None, 0, 0, 0))
    out = lax.map(lambda blk: attend_b(blk[0], blk[1], blk[2], k, v, c), (qb, cqb, pos))
    return jnp.moveaxis(out, 0, 1).reshape(bn, t, D_HEADS, D_HEAD_DIM)


def fox_sample_attend(q, k, v, logf, cache_k, cache_v, cache_logf, layer, page_table):
    def one(args):
        pt, qs, ks, vs, lfs = args
        kp = cache_k[layer, pt].reshape(-1, D_HEADS, D_HEAD_DIM)
        vp = cache_v[layer, pt].reshape(-1, D_HEADS, D_HEAD_DIM)
        lfp = cache_logf[layer, pt].reshape(-1, D_HEADS)
        past = kp.shape[0]
        k_all = jnp.concatenate([kp, ks.astype(kp.dtype)], axis=0)
        v_all = jnp.concatenate([vp, vs.astype(vp.dtype)], axis=0)
        c = jnp.cumsum(jnp.concatenate([lfp.astype(jnp.float32), lfs], axis=0), axis=0)
        q_pos = past + jnp.arange(qs.shape[0])
        return fox_block(qs, c[past:], q_pos, k_all, v_all, c)

    return lax.map(one, (page_table, q, k, v, logf))


def even_mixer(h, conv_prefix, ssm_h0, w_in, ln_g, ln_b, sgu_w, sgu_b, conv_w, conv_b,
               dt_bias, a_log, d_skip, norm_g, w_out):
    bn, t, _ = h.shape
    a_u, a_v, z, xbc, dt_raw = _split(h @ w_in, [A_WIDTH, A_WIDTH, B_WIDTH, B_CONV_DIM, B_HEADS])
    u = jax.nn.gelu(a_u)
    v = layernorm(jax.nn.gelu(a_v), ln_g, ln_b)
    y_a = chunk_sgu(u, v, sgu_w, sgu_b)
    xbc, conv_state = causal_dwconv(xbc, conv_prefix, conv_w, conv_b)
    xbc = jax.nn.silu(xbc)
    xs, bm, cm = _split(xbc, [B_WIDTH, B_GROUPS * B_STATE, B_GROUPS * B_STATE])
    dt = jax.nn.softplus(dt_raw.astype(jnp.float32) + dt_bias.astype(jnp.float32))
    a = -jnp.exp(a_log.astype(jnp.float32))
    xs_h = xs.reshape(bn, t, B_HEADS, B_HEAD_DIM)
    y, h_last = ssd_scan(xs_h, dt, a, bm.reshape(bn, t, B_GROUPS, B_STATE),
                         cm.reshape(bn, t, B_GROUPS, B_STATE), ssm_h0)
    y = y + d_skip.astype(jnp.float32)[:, None] * xs_h.astype(jnp.float32)
    y = y.reshape(bn, t, B_WIDTH) * jax.nn.silu(z.astype(jnp.float32))
    y_b = group_rmsnorm(y, norm_g, B_GROUPS)
    out = jnp.concatenate([y_a.astype(h.dtype), y_b.astype(h.dtype)], axis=-1) @ w_out
    return out, v, conv_state, h_last


def odd_mixer(h, conv_prefix, lru_h0, attend, w_in, conv_w, conv_b, w_a, b_a, w_x, b_x,
              lam, b_f, w_out):
    bn, t, _ = h.shape
    gy, gx, q, k, v, f_raw = _split(h @ w_in, [C_WIDTH, C_WIDTH, D_WIDTH, D_WIDTH, D_WIDTH, D_HEADS])
    gate = jax.nn.gelu(gy).astype(jnp.float32)
    xc, conv_state = causal_dwconv(gx, conv_prefix, conv_w, conv_b)
    xb = xc.reshape(bn, t, C_HEADS, C_BLOCK).astype(jnp.float32)
    r = jax.nn.sigmoid(jnp.einsum('bthi,hij->bthj', xb, w_a.astype(jnp.float32))
                       + b_a.astype(jnp.float32).reshape(C_HEADS, C_BLOCK))
    ig = jax.nn.sigmoid(jnp.einsum('bthi,hij->bthj', xb, w_x.astype(jnp.float32))
                        + b_x.astype(jnp.float32).reshape(C_HEADS, C_BLOCK))
    log_a = -C_GATE_C * r * jax.nn.softplus(-lam.astype(jnp.float32)).reshape(C_HEADS, C_BLOCK)
    a = jnp.exp(log_a)
    bterm = jnp.sqrt(-jnp.expm1(2.0 * log_a)) * (ig * xb)
    hs, h_last = linear_recurrence(a.reshape(bn, t, C_WIDTH), bterm.reshape(bn, t, C_WIDTH),
                                   lru_h0.astype(jnp.float32))
    y_c = hs * gate
    qh = q.reshape(bn, t, D_HEADS, D_HEAD_DIM)
    kh = k.reshape(bn, t, D_HEADS, D_HEAD_DIM)
    vh = v.reshape(bn, t, D_HEADS, D_HEAD_DIM)
    logf = jax.nn.log_sigmoid(f_raw.astype(jnp.float32) + b_f.astype(jnp.float32))
    y_d = attend(qh, kh, vh, logf).reshape(bn, t, D_WIDTH)
    out = jnp.concatenate([y_c.astype(h.dtype), y_d.astype(h.dtype)], axis=-1) @ w_out
    return out, conv_state, h_last, kh, vh, logf


def conv_ffn(h, conv_prefix, w_g, w_u, conv_w, conv_b, w_d):
    g, new_prefix = causal_dwconv(h @ w_g, conv_prefix, conv_w, conv_b)
    return (jax.nn.silu(g) * (h @ w_u)) @ w_d, new_prefix


def setup_inputs(seed: int = 0) -> dict:
    key = jax.random.key(seed)
    keys = jax.random.split(key, 64)
    counter = [0]

    def nk():
        counter[0] += 1
        return keys[counter[0] - 1]

    def nrm(shape, scale):
        return jax.random.normal(nk(), shape, jnp.float32) * scale

    def gain(shape):
        return 1.0 + nrm(shape, 0.02)

    def unif(shape, lo, hi):
        return jax.random.uniform(nk(), shape, jnp.float32, lo, hi)

    n_pages = PAST_LEN // PAGE_SIZE
    n_used = DEC_BATCH * n_pages
    n_pool = n_used + (n_used + 3) // 4
    page_table = jax.random.permutation(nk(), n_pool)[:n_used].reshape(DEC_BATCH, n_pages).astype(jnp.int32)
    dt0 = jnp.exp(unif((N_EVEN, B_HEADS), math.log(1e-3), math.log(1e-1)))
    p_lru = unif((N_ODD, C_WIDTH), 0.9, 0.999) ** (1.0 / C_GATE_C)
    return {
        'x_prompt': nrm((BATCH, SEQ, D_MODEL), 1.0),
        'x_sample': nrm((DEC_BATCH, DEC_SEQ, D_MODEL), 1.0),
        'state_ssm': nrm((N_EVEN, DEC_BATCH, B_HEADS, B_HEAD_DIM, B_STATE), 0.3),
        'state_ssm_conv': nrm((N_EVEN, DEC_BATCH, B_CONV - 1, B_CONV_DIM), 1.0),
        'state_lru': nrm((N_ODD, DEC_BATCH, C_WIDTH), 0.5),
        'state_lru_conv': nrm((N_ODD, DEC_BATCH, C_CONV - 1, C_WIDTH), 1.0),
        'cache_k': nrm((N_ODD, n_pool, PAGE_SIZE, D_HEADS, D_HEAD_DIM), 1.0),
        'cache_v': nrm((N_ODD, n_pool, PAGE_SIZE, D_HEADS, D_HEAD_DIM), 1.0),
        'cache_logf': jax.nn.log_sigmoid(nrm((N_ODD, n_pool, PAGE_SIZE, D_HEADS), 1.0) + 3.0),
        'state_ffn_conv': nrm((DEPTH, DEC_BATCH, FFN_CONV - 1, D_FF), 1.0),
        'page_table': page_table,
        'norm_mix': gain((DEPTH, D_MODEL)),
        'norm_ffn': gain((DEPTH, D_MODEL)),
        'norm_final': gain((D_MODEL,)),
        'w_in_even': nrm((N_EVEN, D_MODEL, IN_EVEN), D_MODEL ** -0.5),
        'sgu_ln_g': gain((N_EVEN, A_WIDTH)),
        'sgu_ln_b': nrm((N_EVEN, A_WIDTH), 0.02),
        'sgu_w': nrm((N_EVEN, A_GROUPS, A_CHUNK, A_CHUNK), A_CHUNK ** -0.5),
        'sgu_b': gain((N_EVEN, A_GROUPS, A_CHUNK)),
        'ssd_conv_w': nrm((N_EVEN, B_CONV, B_CONV_DIM), B_CONV ** -0.5),
        'ssd_conv_b': nrm((N_EVEN, B_CONV_DIM), 0.02),
        'ssd_dt_bias': dt0 + jnp.log(-jnp.expm1(-dt0)),
        'ssd_a_log': jnp.log(unif((N_EVEN, B_HEADS), 1.0, 16.0)),
        'ssd_d': gain((N_EVEN, B_HEADS)),
        'ssd_norm_g': gain((N_EVEN, B_WIDTH)),
        'w_out_even': nrm((N_EVEN, OUT_EVEN, D_MODEL), OUT_EVEN ** -0.5),
        'w_in_odd': nrm((N_ODD, D_MODEL, IN_ODD), D_MODEL ** -0.5),
        'lru_conv_w': nrm((N_ODD, C_CONV, C_WIDTH), C_CONV ** -0.5),
        'lru_conv_b': nrm((N_ODD, C_WIDTH), 0.02),
        'lru_w_a': nrm((N_ODD, C_HEADS, C_BLOCK, C_BLOCK), C_BLOCK ** -0.5),
        'lru_b_a': nrm((N_ODD, C_WIDTH), 0.02),
        'lru_w_x': nrm((N_ODD, C_HEADS, C_BLOCK, C_BLOCK), C_BLOCK ** -0.5),
        'lru_b_x': nrm((N_ODD, C_WIDTH), 0.02),
        'lru_lambda': jnp.log(p_lru) - jnp.log1p(-p_lru),
        'fox_b_f': jnp.linspace(1.0, 6.0, D_HEADS)[None, :] + nrm((N_ODD, D_HEADS), 0.02),
        'w_out_odd': nrm((N_ODD, OUT_ODD, D_MODEL), OUT_ODD ** -0.5),
        'ffn_w_g': nrm((DEPTH, D_MODEL, D_FF), D_MODEL ** -0.5),
        'ffn_w_u': nrm((DEPTH, D_MODEL, D_FF), D_MODEL ** -0.5),
        'ffn_conv_w': nrm((DEPTH, FFN_CONV, D_FF), FFN_CONV ** -0.5),
        'ffn_conv_b': nrm((DEPTH, D_FF), 0.02),
        'ffn_w_d': nrm((DEPTH, D_FF, D_MODEL), D_FF ** -0.5),
    }


def reference(x_prompt, x_sample, state_ssm, state_ssm_conv, state_lru, state_lru_conv,
              cache_k, cache_v, cache_logf, state_ffn_conv, page_table,
              norm_mix, norm_ffn, norm_final,
              w_in_even, sgu_ln_g, sgu_ln_b, sgu_w, sgu_b, ssd_conv_w, ssd_conv_b,
              ssd_dt_bias, ssd_a_log, ssd_d, ssd_norm_g, w_out_even,
              w_in_odd, lru_conv_w, lru_conv_b, lru_w_a, lru_b_a, lru_w_x, lru_b_x,
              lru_lambda, fox_b_f, w_out_odd,
              ffn_w_g, ffn_w_u, ffn_conv_w, ffn_conv_b, ffn_w_d):
    bp, bs = x_prompt.shape[0], x_sample.shape[0]
    xp, xs = x_prompt, x_sample
    sgu_v_l, ssm_p_l, ssm_s_l, sconv_p_l, sconv_s_l = [], [], [], [], []
    lru_p_l, lru_s_l, lconv_p_l, lconv_s_l = [], [], [], []
    kp_l, vp_l, fp_l, ks_l, vs_l, fs_l = [], [], [], [], [], []
    fconv_p_l, fconv_s_l = [], []
    for l in range(DEPTH):
        li = l // 2
        hp = rmsnorm(xp, norm_mix[l])
        hs = rmsnorm(xs, norm_mix[l])
        if l % 2 == 0:
            pe = (w_in_even[li], sgu_ln_g[li], sgu_ln_b[li], sgu_w[li], sgu_b[li], ssd_conv_w[li],
                  ssd_conv_b[li], ssd_dt_bias[li], ssd_a_log[li], ssd_d[li], ssd_norm_g[li], w_out_even[li])
            yp, _, cp, sp = even_mixer(hp, jnp.zeros((bp, B_CONV - 1, B_CONV_DIM), hp.dtype),
                                       jnp.zeros((bp, B_HEADS, B_HEAD_DIM, B_STATE), jnp.float32), *pe)
            ys, vs_new, cs, ss = even_mixer(hs, state_ssm_conv[li], state_ssm[li], *pe)
            sgu_v_l.append(vs_new)
            ssm_p_l.append(sp)
            ssm_s_l.append(ss)
            sconv_p_l.append(cp)
            sconv_s_l.append(cs)
        else:
            po = (w_in_odd[li], lru_conv_w[li], lru_conv_b[li], lru_w_a[li], lru_b_a[li], lru_w_x[li],
                  lru_b_x[li], lru_lambda[li], fox_b_f[li], w_out_odd[li])
            sample_attend = functools.partial(fox_sample_attend, cache_k=cache_k, cache_v=cache_v,
                                              cache_logf=cache_logf, layer=li, page_table=page_table)
            yp, cp, lp, kp_new, vp_new, fp_new = odd_mixer(
                hp, jnp.zeros((bp, C_CONV - 1, C_WIDTH), hp.dtype), jnp.zeros((bp, C_WIDTH), jnp.float32),
                fox_prompt_attend, *po)
            ys, cs, ls, ks_new, vs_new2, fs_new = odd_mixer(
                hs, state_lru_conv[li], state_lru[li], sample_attend, *po)
            lru_p_l.append(lp)
            lru_s_l.append(ls)
            lconv_p_l.append(cp)
            lconv_s_l.append(cs)
            kp_l.append(kp_new)
            vp_l.append(vp_new)
            fp_l.append(fp_new)
            ks_l.append(ks_new)
            vs_l.append(vs_new2)
            fs_l.append(fs_new)
        xp = xp + yp.astype(xp.dtype)
        xs = xs + ys.astype(xs.dtype)
        pf = (ffn_w_g[l], ffn_w_u[l], ffn_conv_w[l], ffn_conv_b[l], ffn_w_d[l])
        yp, fcp = conv_ffn(rmsnorm(xp, norm_ffn[l]), jnp.zeros((bp, FFN_CONV - 1, D_FF), xp.dtype), *pf)
        ys, fcs = conv_ffn(rmsnorm(xs, norm_ffn[l]), state_ffn_conv[l], *pf)
        fconv_p_l.append(fcp)
        fconv_s_l.append(fcs)
        xp = xp + yp.astype(xp.dtype)
        xs = xs + ys.astype(xs.dtype)
    y_prompt = rmsnorm(xp, norm_final)
    y_sample = rmsnorm(xs, norm_final)
    sgu_v_sample = jnp.stack(sgu_v_l)
    ssm_prompt = jnp.stack(ssm_p_l)
    ssm_sample = jnp.stack(ssm_s_l)
    ssm_conv_prompt = jnp.stack(sconv_p_l)
    ssm_conv_sample = jnp.stack(sconv_s_l)
    lru_prompt = jnp.stack(lru_p_l)
    lru_sample = jnp.stack(lru_s_l)
    lru_conv_prompt = jnp.stack(lconv_p_l)
    lru_conv_sample = jnp.stack(lconv_s_l)
    k_prompt = jnp.stack(kp_l)
    v_prompt = jnp.stack(vp_l)
    logf_prompt = jnp.stack(fp_l)
    k_sample = jnp.stack(ks_l)
    v_sample = jnp.stack(vs_l)
    logf_sample = jnp.stack(fs_l)
    ffn_conv_prompt = jnp.stack(fconv_p_l)
    ffn_conv_sample = jnp.stack(fconv_s_l)
    return (y_prompt, y_sample, sgu_v_sample, ssm_prompt, ssm_sample, ssm_conv_prompt, ssm_conv_sample,
            lru_prompt, lru_sample, lru_conv_prompt, lru_conv_sample, k_prompt, v_prompt, logf_prompt,
            k_sample, v_sample, logf_sample, ffn_conv_prompt, ffn_conv_sample)
```

```python
import functools
import math

import jax
import jax.numpy as jnp
import numpy as np
from jax import lax
from jax.experimental import pallas as pl
from jax.experimental.pallas import tpu as pltpu

F32 = jnp.float32
BF16 = jnp.bfloat16

D_MODEL = 2048
PAGE = 128
A_GROUPS = 8
A_GDIM = D_MODEL // A_GROUPS
CHUNK = 128
B_HEADS = 32
B_HDIM = 64
B_GROUPS = 4
B_GW = D_MODEL // B_GROUPS
B_STATE = 128
B_CONV_DIM = D_MODEL + 2 * B_GROUPS * B_STATE
C_HEADS = 8
C_BLOCK = D_MODEL // C_HEADS
C_GATE = 8.0
D_HEADS = 16
D_HDIM = 128
D_FF = 5632
EPS = 1e-6
NEG = -1e30
LOG2E = 1.4426950408889634

LANES = 128
SUBLANES = 8
VMEM_LIMIT = 56 * 1024 * 1024


def _cparams(*sem):
    return pltpu.CompilerParams(dimension_semantics=sem, vmem_limit_bytes=VMEM_LIMIT)


def _gelu(x):
    return jax.nn.gelu(x)


def _silu(x):
    return x * jax.nn.sigmoid(x)


def _softplus(x):
    return jnp.maximum(x, 0.0) + jnp.log1p(jnp.exp(-jnp.abs(x)))


def _log_sigmoid(x):
    return -_softplus(-x)


def _dot(a, b):
    return jnp.dot(a, b, preferred_element_type=F32)


def _dot_nt(a, b):
    return lax.dot_general(a, b, (((1,), (1,)), ((), ())), preferred_element_type=F32)


def _shift_rows(x, k, carry):
    rolled = pltpu.roll(x, k, 0)
    croll = pltpu.roll(carry, k, 0)
    rows = lax.broadcasted_iota(jnp.int32, carry.shape, 0)
    top = jnp.where(rows < k, croll, rolled[:SUBLANES])
    return jnp.concatenate([top, rolled[SUBLANES:]], axis=0)


def _causal_conv(x, carry, w_ref, b_ref):
    width = w_ref.shape[0]
    y = b_ref[...] + x * w_ref[width - 1:width, :]
    for k in range(1, width):
        y = y + _shift_rows(x, k, carry) * w_ref[width - 1 - k:width - k, :]
    return y


def _cumsum_rows(x):
    n = x.shape[0]
    rows = lax.broadcasted_iota(jnp.int32, x.shape, 0)
    s = 1
    while s < n:
        x = x + jnp.where(rows >= s, pltpu.roll(x, s, 0), 0.0)
        s *= 2
    return x


def _expand_heads(x, e_ref):
    e = e_ref[...]
    x1 = x.astype(BF16)
    r1 = x - x1.astype(F32)
    x2 = r1.astype(BF16)
    x3 = (r1 - x2.astype(F32)).astype(BF16)
    return _dot(x1, e) + _dot(x2, e) + _dot(x3, e)


def _rmsnorm(x, g):
    ms = jnp.mean(x * x, axis=-1, keepdims=True)
    return x * lax.rsqrt(ms + EPS) * g


def _rms_matmul_kernel(*refs, n_out, has_side):
    x_ref, g_ref, w_ref = refs[:3]
    pos = 3
    ws_ref = None
    if has_side:
        ws_ref = refs[pos]
        pos += 1
    out_refs = refs[pos:pos + n_out]
    pos += n_out
    side_ref = None
    if has_side:
        side_ref = refs[pos]
        pos += 1
    hn_ref = refs[pos]

    @pl.when(pl.program_id(1) == 0)
    def _():
        hn = _rmsnorm(x_ref[...], g_ref[...]).astype(BF16)
        hn_ref[...] = hn
        if has_side:
            side_ref[...] = _dot(hn, ws_ref[...])

    acc = _dot(hn_ref[...], w_ref[...])
    for o in out_refs:
        o[...] = acc.astype(o.dtype)


def _rms_matmul(x, g, w, out_dtypes, col0, n, w_side=None, tm=1024, tn=1024):
    m, d = x.shape
    tm = min(tm, m)
    tn = min(tn, n)
    assert m % tm == 0 and n % tn == 0 and col0 % tn == 0
    jb = col0 // tn
    has_side = w_side is not None
    in_specs = [
        pl.BlockSpec((tm, d), lambda i, j: (i, 0)),
        pl.BlockSpec((1, d), lambda i, j: (0, 0)),
        pl.BlockSpec((d, tn), lambda i, j: (0, jb + j)),
    ]
    args = [x, g.reshape(1, d), w]
    out_shape = [jax.ShapeDtypeStruct((m, n), dt) for dt in out_dtypes]
    out_specs = [pl.BlockSpec((tm, tn), lambda i, j: (i, j)) for _ in out_dtypes]
    if has_side:
        ns = w_side.shape[1]
        in_specs.append(pl.BlockSpec((d, ns), lambda i, j: (0, 0)))
        args.append(w_side)
        out_shape.append(jax.ShapeDtypeStruct((m, ns), F32))
        out_specs.append(pl.BlockSpec((tm, ns), lambda i, j: (i, 0)))
    outs = pl.pallas_call(
        functools.partial(_rms_matmul_kernel, n_out=len(out_dtypes), has_side=has_side),
        grid=(m // tm, n // tn),
        in_specs=in_specs,
        out_specs=out_specs,
        out_shape=out_shape,
        scratch_shapes=[pltpu.VMEM((tm, d), BF16)],
        compiler_params=_cparams("parallel", "arbitrary"),
        name="rms_matmul",
    )(*args)
    return outs


def _matmul_res_kernel(a1_ref, a2_ref, w1_ref, w2_ref, r_ref, o_ref):
    o_ref[...] = r_ref[...] + _dot(a1_ref[...], w1_ref[...]) + _dot(a2_ref[...], w2_ref[...])


def _matmul_res(a1, a2, w, res, tm=512):
    m, k1 = a1.shape
    k2 = a2.shape[1]
    n = w.shape[1]
    tm = min(tm, m)
    assert m % tm == 0 and k1 == k2
    once = pl.Buffered(1)
    return pl.pallas_call(
        _matmul_res_kernel,
        grid=(m // tm,),
        in_specs=[
            pl.BlockSpec((tm, k1), lambda i: (i, 0)),
            pl.BlockSpec((tm, k2), lambda i: (i, 0)),
            pl.BlockSpec((k1, n), lambda i: (0, 0), pipeline_mode=once),
            pl.BlockSpec((k2, n), lambda i: (1, 0), pipeline_mode=once),
            pl.BlockSpec((tm, n), lambda i: (i, 0)),
        ],
        out_specs=pl.BlockSpec((tm, n), lambda i: (i, 0)),
        out_shape=jax.ShapeDtypeStruct((m, n), F32),
        compiler_params=_cparams("parallel"),
        name="matmul_res",
    )(a1, a2, w, w, res)


def _ffn_prompt_kernel(x_ref, g_ref, wg_ref, wu_ref, cw_ref, cb_ref, wd_ref, gf_ref,
                       o_ref, tail_ref, hn_ref, acc_ref, carry_ref, *, tiles_per_seq, final_norm):
    i = pl.program_id(0)
    j = pl.program_id(1)
    nj = pl.num_programs(1)

    @pl.when(j == 0)
    def _():
        hn_ref[...] = _rmsnorm(x_ref[...], g_ref[...]).astype(BF16)
        acc_ref[...] = jnp.zeros_like(acc_ref)

    @pl.when(i % tiles_per_seq == 0)
    def _():
        carry_ref[j] = jnp.zeros(carry_ref.shape[1:], F32)

    hn = hn_ref[...]
    graw = _dot(hn, wg_ref[...])
    up = _dot(hn, wu_ref[...])
    gc = _causal_conv(graw, carry_ref[j], cw_ref, cb_ref)
    tail = graw[graw.shape[0] - SUBLANES:, :]
    carry_ref[j] = tail
    tail_ref[0] = tail
    act = (_silu(gc) * up).astype(BF16)
    acc_ref[...] += _dot(act, wd_ref[...])

    @pl.when(j == nj - 1)
    def _():
        y = x_ref[...] + acc_ref[...]
        if final_norm:
            y = _rmsnorm(y, gf_ref[...])
        o_ref[...] = y


def _ffn_prompt(x, g, wg, wu, cw, cb, wd, gf, layer, seq, final_norm, tm=512, tf=512):
    m, d = x.shape
    ff = wg.shape[2]
    assert m % tm == 0 and ff % tf == 0 and seq % tm == 0
    tiles_per_seq = seq // tm
    out, tails = pl.pallas_call(
        functools.partial(_ffn_prompt_kernel, tiles_per_seq=tiles_per_seq, final_norm=final_norm),
        grid=(m // tm, ff // tf),
        in_specs=[
            pl.BlockSpec((tm, d), lambda i, j: (i, 0)),
            pl.BlockSpec((1, d), lambda i, j: (0, 0)),
            pl.BlockSpec((None, d, tf), lambda i, j: (layer, 0, j)),
            pl.BlockSpec((None, d, tf), lambda i, j: (layer, 0, j)),
            pl.BlockSpec((cw.shape[0], tf), lambda i, j: (0, j)),
            pl.BlockSpec((1, tf), lambda i, j: (0, j)),
            pl.BlockSpec((None, tf, d), lambda i, j: (layer, j, 0)),
            pl.BlockSpec((1, d), lambda i, j: (0, 0)),
        ],
        out_specs=[
            pl.BlockSpec((tm, d), lambda i, j: (i, 0)),
            pl.BlockSpec((1, SUBLANES, tf), lambda i, j: (i, 0, j)),
        ],
        out_shape=[
            jax.ShapeDtypeStruct((m, d), F32),
            jax.ShapeDtypeStruct((m // tm, SUBLANES, ff), F32),
        ],
        scratch_shapes=[
            pltpu.VMEM((tm, d), BF16),
            pltpu.VMEM((tm, d), F32),
            pltpu.VMEM((ff // tf, SUBLANES, tf), F32),
        ],
        compiler_params=_cparams("arbitrary", "arbitrary"),
        name="ffn_prompt",
    )(x, g.reshape(1, d), wg, wu, cw, cb.reshape(1, ff), wd, gf.reshape(1, d))
    return out, tails[tiles_per_seq - 1::tiles_per_seq]


def _ffn_sample_kernel(x_ref, g_ref, wg_ref, wu_ref, cw_ref, cb_ref, pre_ref, wd_ref, gf_ref,
                       o_ref, npre_ref, hn_ref, acc_ref, *, final_norm):
    j = pl.program_id(0)
    nj = pl.num_programs(0)

    @pl.when(j == 0)
    def _():
        hn_ref[...] = _rmsnorm(x_ref[...], g_ref[...]).astype(BF16)
        acc_ref[...] = jnp.zeros_like(acc_ref)

    hn = hn_ref[...]
    graw = _dot(hn, wg_ref[...])
    up = _dot(hn, wu_ref[...])
    p0 = pre_ref[0]
    p1 = pre_ref[1]
    gc = cb_ref[...] + graw * cw_ref[2:3, :] + p1 * cw_ref[1:2, :] + p0 * cw_ref[0:1, :]
    npre_ref[0] = p1
    npre_ref[1] = graw
    act = (_silu(gc) * up).astype(BF16)
    acc_ref[...] += _dot(act, wd_ref[...])

    @pl.when(j == nj - 1)
    def _():
        y = x_ref[...] + acc_ref[...]
        if final_norm:
            y = _rmsnorm(y, gf_ref[...])
        o_ref[...] = y


def _ffn_sample(x, g, wg, wu, cw, cb, prefix, wd, gf, layer, final_norm, tf=512):
    m, d = x.shape
    ff = wg.shape[2]
    npre = prefix.shape[0]
    out, new_prefix = pl.pallas_call(
        functools.partial(_ffn_sample_kernel, final_norm=final_norm),
        grid=(ff // tf,),
        in_specs=[
            pl.BlockSpec((m, d), lambda j: (0, 0)),
            pl.BlockSpec((1, d), lambda j: (0, 0)),
            pl.BlockSpec((None, d, tf), lambda j: (layer, 0, j)),
            pl.BlockSpec((None, d, tf), lambda j: (layer, 0, j)),
            pl.BlockSpec((cw.shape[0], tf), lambda j: (0, j)),
            pl.BlockSpec((1, tf), lambda j: (0, j)),
            pl.BlockSpec((npre, m, tf), lambda j: (0, 0, j)),
            pl.BlockSpec((None, tf, d), lambda j: (layer, j, 0)),
            pl.BlockSpec((1, d), lambda j: (0, 0)),
        ],
        out_specs=[
            pl.BlockSpec((m, d), lambda j: (0, 0)),
            pl.BlockSpec((npre, m, tf), lambda j: (0, 0, j)),
        ],
        out_shape=[
            jax.ShapeDtypeStruct((m, d), F32),
            jax.ShapeDtypeStruct((npre, m, ff), F32),
        ],
        scratch_shapes=[pltpu.VMEM((m, d), BF16), pltpu.VMEM((m, d), F32)],
        compiler_params=_cparams("arbitrary"),
        name="ffn_sample",
    )(x, g.reshape(1, d), wg, wu, cw, cb.reshape(1, ff), prefix, wd, gf.reshape(1, d))
    return out, new_prefix


def _even_prompt_kernel(au_ref, av_ref, z_ref, xbc_ref, dtr_ref,
                        lng_ref, lnb_ref, sw_ref, sbt_ref, cw_ref, cb_ref, dtb_ref, alog_ref,
                        dsk_ref, ng_ref, e_ref,
                        ya_ref, yb_ref, tail_ref, st_ref,
                        ht_ref, carry_ref):
    c = pl.program_id(1)
    nc = pl.num_programs(1)

    @pl.when(c == 0)
    def _():
        ht_ref[...] = jnp.zeros_like(ht_ref)
        carry_ref[...] = jnp.zeros_like(carry_ref)

    rows = lax.broadcasted_iota(jnp.int32, (CHUNK, CHUNK), 0)
    cols = lax.broadcasted_iota(jnp.int32, (CHUNK, CHUNK), 1)
    tri = rows >= cols

    u = _gelu(au_ref[...].astype(F32))
    gv = _gelu(av_ref[...].astype(F32))
    mu = jnp.mean(gv, axis=-1, keepdims=True)
    xc = gv - mu
    var = jnp.mean(xc * xc, axis=-1, keepdims=True)
    v = xc * lax.rsqrt(var + EPS) * lng_ref[...] + lnb_ref[...]
    vb = v.astype(BF16)
    parts = []
    for g in range(A_GROUPS):
        wg = jnp.where(tri, sw_ref[g], 0.0).astype(BF16)
        parts.append(_dot(wg, vb[:, g * A_GDIM:(g + 1) * A_GDIM]) + sbt_ref[:, g:g + 1])
    ya_ref[...] = (u * jnp.concatenate(parts, axis=1)).astype(BF16)

    xr = xbc_ref[...].astype(F32)
    xbc = _silu(_causal_conv(xr, carry_ref[...], cw_ref, cb_ref))
    carry_ref[...] = xr[CHUNK - SUBLANES:, :]
    xs = xbc[:, :D_MODEL]
    bm = xbc[:, D_MODEL:D_MODEL + B_GROUPS * B_STATE]
    cm = xbc[:, D_MODEL + B_GROUPS * B_STATE:]

    dt = _softplus(dtr_ref[...] + dtb_ref[...])
    a = -jnp.exp(alog_ref[...])
    acs = _cumsum_rows(dt * a)
    ea = jnp.exp(acs)
    te = jnp.exp(acs[CHUNK - 1:CHUNK, :] - acs) * dt
    ea_x = _expand_heads(ea, e_ref)
    te_x = _expand_heads(te, e_ref)
    acs_t = acs.T
    dt_t = dt.T

    lane = lax.broadcasted_iota(jnp.int32, xs.shape, 1)
    lo = (lane & (2 * B_HDIM - 1)) < B_HDIM
    xs_half = (jnp.where(lo, xs, 0.0).astype(BF16), jnp.where(lo, 0.0, xs).astype(BF16))
    xsc_b = (xs * te_x).astype(BF16)
    ht_old = ht_ref[...]
    ht_b = ht_old.astype(BF16)

    ys = []
    states = []
    for g in range(B_GROUPS):
        gsl = slice(g * B_GW, (g + 1) * B_GW)
        bg = bm[:, g * B_STATE:(g + 1) * B_STATE]
        cg_b = cm[:, g * B_STATE:(g + 1) * B_STATE].astype(BF16)
        cb = _dot_nt(cg_b, bg.astype(BF16))
        y_off = _dot(cg_b, ht_b[:, gsl]) * ea_x[:, gsl]
        states.append(_dot(bg.T.astype(BF16), xsc_b[:, gsl]))
        pair_out = []
        for jp in range(B_GW // (2 * B_HDIM)):
            h0 = g * (B_GW // B_HDIM) + 2 * jp
            acc = None
            for e in range(2):
                h = h0 + e
                seg = acs[:, h:h + 1] - acs_t[h:h + 1, :]
                dec = jnp.exp(jnp.where(tri, seg, -jnp.inf))
                mh = (cb * dec * dt_t[h:h + 1, :]).astype(BF16)
                r = _dot(mh, xs_half[e][:, h0 * B_HDIM:(h0 + 2) * B_HDIM])
                acc = r if acc is None else acc + r
            pair_out.append(acc)
        ys.append(jnp.concatenate(pair_out, axis=1) + y_off)
    y = jnp.concatenate(ys, axis=1)
    ht_ref[...] = ea_x[CHUNK - 1:CHUNK, :] * ht_old + jnp.concatenate(states, axis=1)

    y = (y + dsk_ref[...] * xs) * _silu(z_ref[...].astype(F32))
    outs = []
    for g in range(B_GROUPS):
        yg = y[:, g * B_GW:(g + 1) * B_GW]
        ms = jnp.mean(yg * yg, axis=-1, keepdims=True)
        outs.append(yg * lax.rsqrt(ms + EPS))
    yb_ref[...] = (jnp.concatenate(outs, axis=1) * ng_ref[...]).astype(BF16)

    @pl.when(c == nc - 1)
    def _():
        tail_ref[0] = xr[CHUNK - SUBLANES:, :]
        st_ref[0] = ht_ref[...].T


def _even_prompt(proj, dtraw, nb, seq, p):
    n = proj.shape[0]
    nc = seq // CHUNK
    d = D_MODEL

    def row(b, c):
        return b * nc + c

    full = lambda shape: pl.BlockSpec(shape, lambda b, c: (0,) * len(shape))
    ya, yb, tails, state = pl.pallas_call(
        _even_prompt_kernel,
        grid=(nb, nc),
        in_specs=[
            pl.BlockSpec((CHUNK, d), lambda b, c: (row(b, c), 0)),
            pl.BlockSpec((CHUNK, d), lambda b, c: (row(b, c), 1)),
            pl.BlockSpec((CHUNK, d), lambda b, c: (row(b, c), 2)),
            pl.BlockSpec((CHUNK, B_CONV_DIM), lambda b, c: (row(b, c), 2)),
            pl.BlockSpec((CHUNK, LANES), lambda b, c: (row(b, c), 0)),
            full((1, d)), full((1, d)), full((A_GROUPS, CHUNK, CHUNK)), full((CHUNK, A_GROUPS)),
            full((4, B_CONV_DIM)), full((1, B_CONV_DIM)), full((1, LANES)), full((1, LANES)),
            full((1, d)), full((1, d)), full((LANES, d)),
        ],
        out_specs=[
            pl.BlockSpec((CHUNK, d), lambda b, c: (row(b, c), 0)),
            pl.BlockSpec((CHUNK, d), lambda b, c: (row(b, c), 0)),
            pl.BlockSpec((1, SUBLANES, B_CONV_DIM), lambda b, c: (b, 0, 0)),
            pl.BlockSpec((1, d, B_STATE), lambda b, c: (b, 0, 0)),
        ],
        out_shape=[
            jax.ShapeDtypeStruct((n, d), BF16),
            jax.ShapeDtypeStruct((n, d), BF16),
            jax.ShapeDtypeStruct((nb, SUBLANES, B_CONV_DIM), F32),
            jax.ShapeDtypeStruct((nb, d, B_STATE), F32),
        ],
        scratch_shapes=[pltpu.VMEM((B_STATE, d), F32), pltpu.VMEM((SUBLANES, B_CONV_DIM), F32)],
        compiler_params=_cparams("parallel", "arbitrary"),
        name="even_prompt",
    )(proj, proj, proj, proj, dtraw,
      p["ln_g"], p["ln_b"], p["sgu_w"], p["sgu_bt"], p["conv_w"], p["conv_b"], p["dt_bias"], p["a_log"],
      p["d_skip"], p["norm_g"], p["expand"])
    return ya, yb, tails, state


def _even_sample_a_kernel(proj_ref, dtr_ref, pre_ref, lng_ref, lnb_ref, w0_ref, b0_ref, cw_ref, cb_ref,
                          dtb_ref, alog_ref,
                          ya_ref, v_ref, npre_ref, xs_ref, bm_ref, cm_ref, dt_ref, dec_ref):
    d = D_MODEL
    u = _gelu(proj_ref[:, 0:d])
    gv = _gelu(proj_ref[:, d:2 * d])
    mu = jnp.mean(gv, axis=-1, keepdims=True)
    xc = gv - mu
    var = jnp.mean(xc * xc, axis=-1, keepdims=True)
    v = xc * lax.rsqrt(var + EPS) * lng_ref[...] + lnb_ref[...]
    v_ref[...] = v
    ya_ref[...] = (u * (w0_ref[...] * v + b0_ref[...])).astype(BF16)

    xr = proj_ref[:, 3 * d:3 * d + B_CONV_DIM]
    p0, p1, p2 = pre_ref[0], pre_ref[1], pre_ref[2]
    y = (cb_ref[...] + xr * cw_ref[3:4, :] + p2 * cw_ref[2:3, :] + p1 * cw_ref[1:2, :]
         + p0 * cw_ref[0:1, :])
    npre_ref[0] = p1
    npre_ref[1] = p2
    npre_ref[2] = xr
    xbc = _silu(y)
    xs_ref[...] = xbc[:, :d]
    bm_ref[...] = xbc[:, d:d + B_GROUPS * B_STATE]
    cm_ref[...] = xbc[:, d + B_GROUPS * B_STATE:]
    dt = _softplus(dtr_ref[...] + dtb_ref[...])
    dt_ref[...] = dt
    dec_ref[...] = jnp.exp(dt * (-jnp.exp(alog_ref[...])))


def _even_sample_a(proj, dtraw, prefix, p):
    m = proj.shape[0]
    d = D_MODEL
    gs = B_GROUPS * B_STATE
    return pl.pallas_call(
        _even_sample_a_kernel,
        out_shape=[
            jax.ShapeDtypeStruct((m, d), BF16),
            jax.ShapeDtypeStruct((m, d), F32),
            jax.ShapeDtypeStruct(prefix.shape, F32),
            jax.ShapeDtypeStruct((m, d), F32),
            jax.ShapeDtypeStruct((m, gs), F32),
            jax.ShapeDtypeStruct((m, gs), F32),
            jax.ShapeDtypeStruct((m, LANES), F32),
            jax.ShapeDtypeStruct((m, LANES), F32),
        ],
        compiler_params=pltpu.CompilerParams(vmem_limit_bytes=VMEM_LIMIT),
        name="even_sample_a",
    )(proj, dtraw, prefix, p["ln_g"], p["ln_b"], p["sgu_w0"], p["sgu_b0"], p["conv_w"], p["conv_b"],
      p["dt_bias"], p["a_log"])


def _ssm_state_sample_kernel(h_ref, x_ref, dt_ref, dec_ref, bm_ref, cm_ref, hn_ref, y_ref):
    for g in range(B_GROUPS):
        sl = slice(g * B_GW, (g + 1) * B_GW)
        hn = dec_ref[0, sl, :] * h_ref[0, sl, :] + (dt_ref[0, sl, :] * x_ref[0, sl, :]) * bm_ref[0, g:g + 1, :]
        hn_ref[0, sl, :] = hn
        y_ref[0, sl, :] = jnp.sum(hn * cm_ref[0, g:g + 1, :], axis=-1, keepdims=True)


def _ssm_state_sample(h0, xcol, dtcol, deccol, bm, cm):
    m, d, ns = h0.shape
    col = pl.BlockSpec((1, d, 1), lambda b: (b, 0, 0))
    grp = pl.BlockSpec((1, B_GROUPS, ns), lambda b: (b, 0, 0))
    st = pl.BlockSpec((1, d, ns), lambda b: (b, 0, 0))
    return pl.pallas_call(
        _ssm_state_sample_kernel,
        grid=(m,),
        in_specs=[st, col, col, col, grp, grp],
        out_specs=[st, col],
        out_shape=[jax.ShapeDtypeStruct((m, d, ns), F32), jax.ShapeDtypeStruct((m, d, 1), F32)],
        compiler_params=_cparams("parallel"),
        name="ssm_state_sample",
    )(h0, xcol, dtcol, deccol, bm, cm)


def _even_sample_c_kernel(y_ref, xs_ref, z_ref, dsk_ref, ng_ref, yb_ref):
    y = (y_ref[...] + dsk_ref[...] * xs_ref[...]) * _silu(z_ref[...])
    outs = []
    for g in range(B_GROUPS):
        yg = y[:, g * B_GW:(g + 1) * B_GW]
        ms = jnp.mean(yg * yg, axis=-1, keepdims=True)
        outs.append(yg * lax.rsqrt(ms + EPS))
    yb_ref[...] = (jnp.concatenate(outs, axis=1) * ng_ref[...]).astype(BF16)


def _even_sample_c(y, xs, z, p):
    return pl.pallas_call(
        _even_sample_c_kernel,
        out_shape=jax.ShapeDtypeStruct(y.shape, BF16),
        name="even_sample_c",
    )(y, xs, z, p["d_skip"], p["norm_g"])


def _lru_gates(xc, wa_ref, ba_ref, wx_ref, bx_ref, lam_ref):
    xcb = xc.astype(BF16)
    rp, ip = [], []
    for h in range(C_HEADS):
        xh = xcb[:, h * C_BLOCK:(h + 1) * C_BLOCK]
        rp.append(_dot(xh, wa_ref[h]))
        ip.append(_dot(xh, wx_ref[h]))
    r = jax.nn.sigmoid(jnp.concatenate(rp, axis=1) + ba_ref[...])
    ig = jax.nn.sigmoid(jnp.concatenate(ip, axis=1) + bx_ref[...])
    log_a = -C_GATE * r * _softplus(-lam_ref[...])
    a = jnp.exp(log_a)
    bt = jnp.sqrt(-jnp.tanh(log_a) * (a * a + 1.0)) * (ig * xc)
    return a, bt


def _lru_prompt_kernel(gy_ref, gx_ref, fr_ref, cw_ref, cb_ref, wa_ref, ba_ref, wx_ref, bx_ref, lam_ref, bf_ref,
                       yc_ref, lf_ref, ccol_ref, crow_ref, last_ref, tail_ref,
                       a_s, b_s, h_s, hc_ref, cc_ref, fc_ref):
    t = pl.program_id(1)
    nt = pl.num_programs(1)
    tm = gy_ref.shape[0]

    @pl.when(t == 0)
    def _():
        hc_ref[...] = jnp.zeros_like(hc_ref)
        cc_ref[...] = jnp.zeros_like(cc_ref)
        fc_ref[...] = jnp.zeros_like(fc_ref)

    xr = gx_ref[...].astype(F32)
    xc = _causal_conv(xr, cc_ref[...], cw_ref, cb_ref)
    cc_ref[...] = xr[tm - SUBLANES:, :]
    a, bt = _lru_gates(xc, wa_ref, ba_ref, wx_ref, bx_ref, lam_ref)
    a_s[...] = a
    b_s[...] = bt

    def body(i, h):
        h = a_s[pl.ds(i, 1), :] * h + b_s[pl.ds(i, 1), :]
        h_s[pl.ds(i, 1), :] = h
        return h

    h = lax.fori_loop(0, tm, body, hc_ref[0:1, :], unroll=8)
    hc_ref[0:1, :] = h
    yc_ref[...] = (h_s[...] * _gelu(gy_ref[...].astype(F32))).astype(BF16)

    lf = _log_sigmoid(fr_ref[...] + bf_ref[...])
    cs = _cumsum_rows(lf) + fc_ref[0:1, :]
    fc_ref[0:1, :] = cs[tm - 1:tm, :]
    lf_ref[...] = lf
    ccol_ref[...] = cs
    crow_ref[0] = cs.T[:D_HEADS, :]

    @pl.when(t == nt - 1)
    def _():
        last_ref[0] = h
        tail_ref[0] = xr[tm - SUBLANES:, :]


def _lru_prompt(proj, fraw, nb, seq, p, tm=256):
    n = proj.shape[0]
    d = D_MODEL
    nt = seq // tm

    def row(b, t):
        return b * nt + t

    full = lambda shape: pl.BlockSpec(shape, lambda b, t: (0,) * len(shape))
    return pl.pallas_call(
        _lru_prompt_kernel,
        grid=(nb, nt),
        in_specs=[
            pl.BlockSpec((tm, d), lambda b, t: (row(b, t), 0)),
            pl.BlockSpec((tm, d), lambda b, t: (row(b, t), 1)),
            pl.BlockSpec((tm, LANES), lambda b, t: (row(b, t), 0)),
            full((4, d)), full((1, d)),
            full((C_HEADS, C_BLOCK, C_BLOCK)), full((1, d)),
            full((C_HEADS, C_BLOCK, C_BLOCK)), full((1, d)),
            full((1, d)), full((1, LANES)),
        ],
        out_specs=[
            pl.BlockSpec((tm, d), lambda b, t: (row(b, t), 0)),
            pl.BlockSpec((tm, LANES), lambda b, t: (row(b, t), 0)),
            pl.BlockSpec((tm, LANES), lambda b, t: (row(b, t), 0)),
            pl.BlockSpec((1, D_HEADS, tm), lambda b, t: (b, 0, t)),
            pl.BlockSpec((1, 1, d), lambda b, t: (b, 0, 0)),
            pl.BlockSpec((1, SUBLANES, d), lambda b, t: (b, 0, 0)),
        ],
        out_shape=[
            jax.ShapeDtypeStruct((n, d), BF16),
            jax.ShapeDtypeStruct((n, LANES), F32),
            jax.ShapeDtypeStruct((n, LANES), F32),
            jax.ShapeDtypeStruct((nb, D_HEADS, seq), F32),
            jax.ShapeDtypeStruct((nb, 1, d), F32),
            jax.ShapeDtypeStruct((nb, SUBLANES, d), F32),
        ],
        scratch_shapes=[
            pltpu.VMEM((tm, d), F32), pltpu.VMEM((tm, d), F32), pltpu.VMEM((tm, d), F32),
            pltpu.VMEM((SUBLANES, d), F32), pltpu.VMEM((SUBLANES, d), F32), pltpu.VMEM((SUBLANES, LANES), F32),
        ],
        compiler_params=_cparams("parallel", "arbitrary"),
        name="lru_prompt",
    )(proj, proj, fraw, p["conv_w"], p["conv_b"], p["w_a"], p["b_a"], p["w_x"], p["b_x"], p["lam"], p["b_f"])


def _attn_prompt_kernel(q_ref, k_ref, v_ref, cq_ref, ck_ref, o_ref, m_ref, l_ref, acc_ref, cqb_ref):
    qi = pl.program_id(1)
    ki = pl.program_id(2)
    tq = q_ref.shape[0]
    tk = k_ref.shape[0]
    rep = tk // LANES
    scale2 = (D_HDIM ** -0.5) * LOG2E

    @pl.when(ki == 0)
    def _():
        m_ref[...] = jnp.full_like(m_ref, NEG)
        l_ref[...] = jnp.zeros_like(l_ref)
        acc_ref[...] = jnp.zeros_like(acc_ref)
        cq2 = cq_ref[...] * LOG2E
        for h in range(D_HEADS):
            cqb_ref[h] = jnp.broadcast_to(cq2[:, h:h + 1], (tq, LANES))

    def tile(on_diagonal):
        if on_diagonal:
            causal = (lax.broadcasted_iota(jnp.int32, (tq, tk), 0)
                      >= lax.broadcasted_iota(jnp.int32, (tq, tk), 1))
        ck2 = ck_ref[0] * LOG2E
        for h in range(D_HEADS):
            hs = slice(h * D_HDIM, (h + 1) * D_HDIM)
            t = _dot_nt(q_ref[:, hs], k_ref[:, hs]) * scale2 - ck2[h:h + 1, :]
            if on_diagonal:
                t = jnp.where(causal, t, NEG)
            cqh = cqb_ref[h]
            m_old = m_ref[h]
            rmax = jnp.broadcast_to(jnp.max(t, axis=-1, keepdims=True), (tq, LANES))
            m_new = jnp.maximum(m_old, rmax + cqh)
            alpha = jnp.exp2(m_old - m_new)
            pm = jnp.exp2(t - pltpu.repeat(m_new - cqh, rep, axis=1))
            rsum = jnp.broadcast_to(jnp.sum(pm, axis=-1, keepdims=True), (tq, LANES))
            l_ref[h] = alpha * l_ref[h] + rsum
            acc_ref[:, hs] = alpha * acc_ref[:, hs] + _dot(pm.astype(BF16), v_ref[:, hs])
            m_ref[h] = m_new

    @pl.when(ki < qi)
    def _():
        tile(False)

    @pl.when(ki == qi)
    def _():
        tile(True)
        for h in range(D_HEADS):
            hs = slice(h * D_HDIM, (h + 1) * D_HDIM)
            o_ref[:, hs] = (acc_ref[:, hs] * (1.0 / l_ref[h])).astype(BF16)


def _attn_prompt(proj, kb, vb, ccol, crow, nb, seq, tq=512):
    n = proj.shape[0]
    d = D_MODEL
    tq = min(tq, seq)
    nq = seq // tq
    return pl.pallas_call(
        _attn_prompt_kernel,
        grid=(nb, nq, nq),
        in_specs=[
            pl.BlockSpec((tq, d), lambda b, qi, ki: (b * nq + qi, 2)),
            pl.BlockSpec((tq, d), lambda b, qi, ki: (b * nq + jnp.minimum(ki, qi), 0)),
            pl.BlockSpec((tq, d), lambda b, qi, ki: (b * nq + jnp.minimum(ki, qi), 0)),
            pl.BlockSpec((tq, LANES), lambda b, qi, ki: (b * nq + qi, 0)),
            pl.BlockSpec((1, D_HEADS, tq), lambda b, qi, ki: (b, 0, jnp.minimum(ki, qi))),
        ],
        out_specs=pl.BlockSpec((tq, d), lambda b, qi, ki: (b * nq + qi, 0)),
        out_shape=jax.ShapeDtypeStruct((n, d), BF16),
        scratch_shapes=[pltpu.VMEM((D_HEADS, tq, LANES), F32), pltpu.VMEM((D_HEADS, tq, LANES), F32),
                        pltpu.VMEM((tq, d), F32), pltpu.VMEM((D_HEADS, tq, LANES), F32)],
        compiler_params=_cparams("parallel", "parallel", "arbitrary"),
        name="attn_prompt",
    )(proj, kb, vb, ccol, crow)


def _odd_sample_kernel(gy_ref, gx_ref, fr_ref, pre_ref, h0_ref, cw_ref, cb_ref, wa_ref, ba_ref, wx_ref, bx_ref,
                       lam_ref, bf_ref, yc_ref, hn_ref, npre_ref, lf_ref):
    xr = gx_ref[...]
    p0, p1, p2 = pre_ref[0], pre_ref[1], pre_ref[2]
    xc = (cb_ref[...] + xr * cw_ref[3:4, :] + p2 * cw_ref[2:3, :] + p1 * cw_ref[1:2, :]
          + p0 * cw_ref[0:1, :])
    npre_ref[0] = p1
    npre_ref[1] = p2
    npre_ref[2] = xr
    a, bt = _lru_gates(xc, wa_ref, ba_ref, wx_ref, bx_ref, lam_ref)
    hn = a * h0_ref[...] + bt
    hn_ref[...] = hn
    yc_ref[...] = (hn * _gelu(gy_ref[...])).astype(BF16)
    lf_ref[...] = _log_sigmoid(fr_ref[...] + bf_ref[...])


def _odd_sample(gy, gx, fraw, prefix, h0, p):
    m, d = gy.shape
    return pl.pallas_call(
        _odd_sample_kernel,
        out_shape=[
            jax.ShapeDtypeStruct((m, d), BF16),
            jax.ShapeDtypeStruct((m, d), F32),
            jax.ShapeDtypeStruct(prefix.shape, F32),
            jax.ShapeDtypeStruct((m, LANES), F32),
        ],
        compiler_params=pltpu.CompilerParams(vmem_limit_bytes=VMEM_LIMIT),
        name="odd_sample",
    )(gy, gx, fraw, prefix, h0, p["conv_w"], p["conv_b"], p["w_a"], p["b_a"], p["w_x"], p["b_x"], p["lam"],
      p["b_f"])


def _page_bias_kernel(lf_ref, ex_ref, tot_ref):
    x0 = lf_ref[...]
    n = x0.shape[1]
    lane = lax.broadcasted_iota(jnp.int32, x0.shape, 1)
    ex = jnp.where(lane + D_HEADS < n, pltpu.roll(x0, n - D_HEADS, 1), 0.0)
    tot = x0
    s = D_HEADS
    while s < n:
        ex = ex + jnp.where(lane + s < n, pltpu.roll(ex, n - s, 1), 0.0)
        tot = tot + pltpu.roll(tot, n - s, 1)
        s *= 2
    ex_ref[...] = ex
    tot_ref[...] = tot


def _page_bias(lf_flat, tr=256):
    r, n = lf_flat.shape
    tr = math.gcd(r, tr)
    spec = pl.BlockSpec((tr, n), lambda i: (i, 0))
    return pl.pallas_call(
        _page_bias_kernel,
        grid=(r // tr,),
        in_specs=[spec],
        out_specs=[spec, spec],
        out_shape=[jax.ShapeDtypeStruct((r, n), F32), jax.ShapeDtypeStruct((r, n), F32)],
        compiler_params=_cparams("parallel"),
        name="page_bias",
    )(lf_flat)


def _paged_attn_kernel(pt_ref, q_ref, kn_ref, vn_ref, lfn_ref, *rest, pages):
    k_refs = rest[0:pages]
    v_refs = rest[pages:2 * pages]
    ex_refs = rest[2 * pages:3 * pages]
    tot_refs = rest[3 * pages:4 * pages]
    o_ref, m_ref, l_ref, acc_ref, carry_ref = rest[4 * pages:]
    p = pl.program_id(1)
    npg = pl.num_programs(1)
    scale = D_HDIM ** -0.5
    q = q_ref[0]

    @pl.when(p == 0)
    def _():
        s0 = jnp.sum(q * kn_ref[0], axis=-1, keepdims=True) * scale
        m_ref[...] = jnp.broadcast_to(s0, m_ref.shape)
        l_ref[...] = jnp.ones_like(l_ref)
        acc_ref[...] = vn_ref[0]
        carry_ref[0:1, :] = lfn_ref[0]

    nrow = k_refs[0].shape[1]
    qb = q.astype(BF16)
    lane = lax.broadcasted_iota(jnp.int32, (D_HEADS, nrow), 1)
    sub = lax.broadcasted_iota(jnp.int32, (D_HEADS, nrow), 0)
    own = (lane & (D_HEADS - 1)) == sub
    carry = carry_ref[0:1, :]
    scores = []
    for j in range(pages):
        s = _dot_nt(qb, k_refs[j][0].astype(BF16)) * scale
        scores.append(jnp.where(own, s + (ex_refs[j][0] + carry), NEG))
        carry = carry + tot_refs[j][0]
    carry_ref[0:1, :] = carry
    mx = jnp.max(scores[0], axis=-1, keepdims=True)
    for s in scores[1:]:
        mx = jnp.maximum(mx, jnp.max(s, axis=-1, keepdims=True))
    m_old = m_ref[:, 0:1]
    m_new = jnp.maximum(m_old, mx)
    alpha = jnp.exp(m_old - m_new)
    lsum = None
    pv = None
    for j in range(pages):
        pm = jnp.exp(scores[j] - m_new)
        ls = jnp.sum(pm, axis=-1, keepdims=True)
        r = _dot(pm.astype(BF16), v_refs[j][0].astype(BF16))
        lsum = ls if lsum is None else lsum + ls
        pv = r if pv is None else pv + r
    l_ref[...] = alpha * l_ref[...] + lsum
    acc_ref[...] = alpha * acc_ref[...] + pv
    m_ref[...] = jnp.broadcast_to(m_new, m_ref.shape)

    @pl.when(p == npg - 1)
    def _():
        o_ref[0] = acc_ref[...] / l_ref[...]


def _paged_attn(page_table, q, kn, vn, lfn_rep, kc, vc, ex, tot, page_base, pages=4):
    m, npg = page_table.shape
    nrow = kc.shape[1]
    pages = math.gcd(npg, pages)

    def pg(j):
        return lambda b, p, pt: (page_base + pt[b, npg - 1 - (p * pages + j)], 0, 0)

    tok = pl.BlockSpec((1, D_HEADS, D_HDIM), lambda b, p, pt: (b, 0, 0))
    flat = pl.BlockSpec((1, 1, nrow), lambda b, p, pt: (b, 0, 0))
    big = [pl.BlockSpec((1, nrow, D_HDIM), pg(j)) for j in range(pages)]
    small = [pl.BlockSpec((1, 1, nrow), pg(j)) for j in range(pages)]
    grid_spec = pltpu.PrefetchScalarGridSpec(
        num_scalar_prefetch=1,
        grid=(m, npg // pages),
        in_specs=[tok, tok, tok, flat] + big + big + small + small,
        out_specs=tok,
        scratch_shapes=[
            pltpu.VMEM((D_HEADS, D_HDIM), F32), pltpu.VMEM((D_HEADS, D_HDIM), F32),
            pltpu.VMEM((D_HEADS, D_HDIM), F32), pltpu.VMEM((SUBLANES, nrow), F32),
        ],
    )
    return pl.pallas_call(
        functools.partial(_paged_attn_kernel, pages=pages),
        grid_spec=grid_spec,
        out_shape=jax.ShapeDtypeStruct((m, D_HEADS, D_HDIM), F32),
        compiler_params=_cparams("parallel", "arbitrary"),
        name="paged_attn",
    )(page_table, q, kn, vn, lfn_rep, *([kc] * pages), *([vc] * pages), *([ex] * pages), *([tot] * pages))


def _pad_lanes(v, n=LANES):
    v = v.reshape(1, -1).astype(F32)
    return jnp.pad(v, ((0, 0), (0, n - v.shape[1])))


def _pad_cols(w, n=LANES):
    return jnp.pad(w, ((0, 0), (0, n - w.shape[1])))


def _row(v):
    return v.reshape(1, -1).astype(F32)


def _expand_matrix():
    e = np.zeros((LANES, D_MODEL), np.float32)
    for h in range(B_HEADS):
        e[h, h * B_HDIM:(h + 1) * B_HDIM] = 1.0
    return jnp.asarray(e, BF16)


def kernel(x_prompt, x_sample, state_ssm, state_ssm_conv, state_lru, state_lru_conv, cache_k, cache_v, cache_logf, state_ffn_conv, page_table, norm_mix, norm_ffn, norm_final, w_in_even, sgu_ln_g, sgu_ln_b, sgu_w, sgu_b, ssd_conv_w, ssd_conv_b, ssd_dt_bias, ssd_a_log, ssd_d, ssd_norm_g, w_out_even, w_in_odd, lru_conv_w, lru_conv_b, lru_w_a, lru_b_a, lru_w_x, lru_b_x, lru_lambda, fox_b_f, w_out_odd, ffn_w_g, ffn_w_u, ffn_conv_w, ffn_conv_b, ffn_w_d):
    nb, seq, d = x_prompt.shape
    ms = x_sample.shape[0]
    depth = norm_mix.shape[0]
    n_pool = cache_k.shape[1]
    xp = x_prompt.reshape(nb * seq, d)
    xs = x_sample.reshape(ms, d)
    gs = B_GROUPS * B_STATE
    main_even = 3 * d + B_CONV_DIM

    res = {k: [] for k in ("sgu_v", "ssm_p", "ssm_s", "sconv_p", "sconv_s", "lru_p", "lru_s", "lconv_p",
                           "lconv_s", "k_p", "v_p", "f_p", "k_s", "v_s", "f_s", "fconv_p", "fconv_s")}

    wg_all = ffn_w_g.astype(BF16)
    wu_all = ffn_w_u.astype(BF16)
    wd_all = ffn_w_d.astype(BF16)

    for l in range(depth):
        li = l // 2
        if l % 2 == 0:
            w_in = w_in_even[li]
            w_bf = w_in.astype(BF16)
            w_dt = _pad_cols(w_in[:, main_even:]).astype(BF16)
            p = {
                "ln_g": _row(sgu_ln_g[li]), "ln_b": _row(sgu_ln_b[li]),
                "sgu_w": sgu_w[li].astype(F32), "sgu_bt": sgu_b[li].astype(F32).T,
                "sgu_w0": _row(jnp.repeat(sgu_w[li][:, 0, 0], A_GDIM)),
                "sgu_b0": _row(jnp.repeat(sgu_b[li][:, 0], A_GDIM)),
                "conv_w": ssd_conv_w[li].astype(F32), "conv_b": _row(ssd_conv_b[li]),
                "dt_bias": _pad_lanes(ssd_dt_bias[li]), "a_log": _pad_lanes(ssd_a_log[li]),
                "d_skip": _row(jnp.repeat(ssd_d[li], B_HDIM)), "norm_g": _row(ssd_norm_g[li]),
                "expand": _expand_matrix(),
            }
            w_out = w_out_even[li].astype(BF16)

            proj, dtraw = _rms_matmul(xp, norm_mix[l], w_bf, (BF16,), 0, main_even, w_side=w_dt)
            ya, yb, tails, state = _even_prompt(proj, dtraw, nb, seq, p)
            xp = _matmul_res(ya, yb, w_out, xp)
            res["ssm_p"].append(state.reshape(nb, B_HEADS, B_HDIM, B_STATE))
            res["sconv_p"].append(tails[:, SUBLANES - 3:, :])

            proj_s, dtraw_s = _rms_matmul(xs, norm_mix[l], w_bf, (F32,), 0, main_even, w_side=w_dt)
            prefix = jnp.swapaxes(state_ssm_conv[li], 0, 1)
            ya_s, v_s, npre, xs_s, bm_s, cm_s, dt_s, dec_s = _even_sample_a(proj_s, dtraw_s, prefix, p)
            dtcol = jnp.repeat(dt_s[:, :B_HEADS], B_HDIM, axis=1).reshape(ms, d, 1)
            deccol = jnp.repeat(dec_s[:, :B_HEADS], B_HDIM, axis=1).reshape(ms, d, 1)
            hnew, ycol = _ssm_state_sample(
                state_ssm[li].reshape(ms, d, B_STATE), xs_s.reshape(ms, d, 1), dtcol, deccol,
                bm_s.reshape(ms, B_GROUPS, B_STATE), cm_s.reshape(ms, B_GROUPS, B_STATE))
            yb_s = _even_sample_c(ycol.reshape(ms, d), xs_s, proj_s[:, 2 * d:3 * d], p)
            xs = _matmul_res(ya_s, yb_s, w_out, xs)
            res["sgu_v"].append(v_s.reshape(ms, 1, d))
            res["ssm_s"].append(hnew.reshape(ms, B_HEADS, B_HDIM, B_STATE))
            res["sconv_s"].append(jnp.swapaxes(npre, 0, 1))
        else:
            w_in = w_in_odd[li]
            w_bf = w_in.astype(BF16)
            w_f = _pad_cols(w_in[:, 5 * d:]).astype(BF16)
            p = {
                "conv_w": lru_conv_w[li].astype(F32), "conv_b": _row(lru_conv_b[li]),
                "w_a": lru_w_a[li].astype(BF16), "b_a": _row(lru_b_a[li]),
                "w_x": lru_w_x[li].astype(BF16), "b_x": _row(lru_b_x[li]),
                "lam": _row(lru_lambda[li]), "b_f": _pad_lanes(fox_b_f[li]),
            }
            w_out = w_out_odd[li].astype(BF16)

            proj, fraw = _rms_matmul(xp, norm_mix[l], w_bf, (BF16,), 0, 3 * d, w_side=w_f)
            kf, kb = _rms_matmul(xp, norm_mix[l], w_bf, (F32, BF16), 3 * d, d, tm=512, tn=d)
            vf, vb = _rms_matmul(xp, norm_mix[l], w_bf, (F32, BF16), 4 * d, d, tm=512, tn=d)
            yc, lf, ccol, crow, last, ctail = _lru_prompt(proj, fraw, nb, seq, p)
            yd = _attn_prompt(proj, kb, vb, ccol, crow, nb, seq)
            xp = _matmul_res(yc, yd, w_out, xp)
            res["lru_p"].append(last.reshape(nb, d))
            res["lconv_p"].append(ctail[:, SUBLANES - 3:, :])
            res["k_p"].append(kf.reshape(nb, seq, D_HEADS, D_HDIM))
            res["v_p"].append(vf.reshape(nb, seq, D_HEADS, D_HDIM))
            res["f_p"].append(lf[:, :D_HEADS].reshape(nb, seq, D_HEADS))

            proj_s, fraw_s = _rms_matmul(xs, norm_mix[l], w_bf, (F32,), 0, 5 * d, w_side=w_f)
            prefix = jnp.swapaxes(state_lru_conv[li], 0, 1)
            yc_s, hn_s, npre, lf_s = _odd_sample(proj_s[:, :d], proj_s[:, d:2 * d], fraw_s, prefix,
                                                 state_lru[li].astype(F32), p)
            q_s = proj_s[:, 2 * d:3 * d].reshape(ms, D_HEADS, D_HDIM)
            k_s = proj_s[:, 3 * d:4 * d].reshape(ms, D_HEADS, D_HDIM)
            v_s = proj_s[:, 4 * d:5 * d].reshape(ms, D_HEADS, D_HDIM)
            lfn = lf_s[:, :D_HEADS]
            nrow = PAGE * D_HEADS
            ex, tot = _page_bias(cache_logf[li].reshape(n_pool, nrow).astype(F32))
            yd_s = _paged_attn(
                page_table, q_s, k_s, v_s, jnp.tile(lfn, (1, PAGE)).reshape(ms, 1, nrow),
                cache_k.reshape(-1, nrow, D_HDIM), cache_v.reshape(-1, nrow, D_HDIM),
                ex.reshape(n_pool, 1, nrow), tot.reshape(n_pool, 1, nrow), li * n_pool)
            xs = _matmul_res(yc_s, yd_s.reshape(ms, d).astype(BF16), w_out, xs)
            res["lru_s"].append(hn_s)
            res["lconv_s"].append(jnp.swapaxes(npre, 0, 1))
            res["k_s"].append(k_s.reshape(ms, 1, D_HEADS, D_HDIM))
            res["v_s"].append(v_s.reshape(ms, 1, D_HEADS, D_HDIM))
            res["f_s"].append(lfn.reshape(ms, 1, D_HEADS))

        last_layer = l == depth - 1
        cw = ffn_conv_w[l].astype(F32)
        xp, ftails = _ffn_prompt(xp, norm_ffn[l], wg_all, wu_all, cw, ffn_conv_b[l], wd_all, norm_final, l, seq,
                                 last_layer)
        xs, fpre = _ffn_sample(xs, norm_ffn[l], wg_all, wu_all, cw, ffn_conv_b[l],
                               jnp.swapaxes(state_ffn_conv[l], 0, 1).astype(F32), wd_all, norm_final, l,
                               last_layer)
        res["fconv_p"].append(ftails[:, SUBLANES - 2:, :])
        res["fconv_s"].append(jnp.swapaxes(fpre, 0, 1))

    st = lambda k: jnp.stack(res[k])
    return (xp.reshape(nb, seq, d), xs.reshape(ms, 1, d), st("sgu_v"), st("ssm_p"), st("ssm_s"), st("sconv_p"),
            st("sconv_s"), st("lru_p"), st("lru_s"), st("lconv_p"), st("lconv_s"), st("k_p"), st("v_p"),
            st("f_p"), st("k_s"), st("v_s"), st("f_s"), st("fconv_p"), st("fconv_s"))
```

```python
import functools
import math

import jax
import jax.numpy as jnp
import numpy as np
from jax import lax
from jax.experimental import pallas as pl
from jax.experimental.pallas import tpu as pltpu

F32 = jnp.float32
BF16 = jnp.bfloat16

D_MODEL = 2048
PAGE = 128
A_GROUPS = 8
A_GDIM = D_MODEL // A_GROUPS
CHUNK = 128
B_HEADS = 32
B_HDIM = 64
B_GROUPS = 4
B_GW = D_MODEL // B_GROUPS
B_STATE = 128
B_CONV_DIM = D_MODEL + 2 * B_GROUPS * B_STATE
C_HEADS = 8
C_BLOCK = D_MODEL // C_HEADS
C_GATE = 8.0
D_HEADS = 16
D_HDIM = 128
D_FF = 5632
EPS = 1e-6
NEG = -1e30
LOG2E = 1.4426950408889634
GELU_C1 = 2.0 * math.sqrt(2.0 / math.pi)
GELU_C3 = GELU_C1 * 0.044715

LANES = 128
SUBLANES = 8
VMEM_LIMIT = 56 * 1024 * 1024


def _cparams(*sem):
    return pltpu.CompilerParams(dimension_semantics=sem, vmem_limit_bytes=VMEM_LIMIT)


def _gelu(x):
    y2 = x * (GELU_C1 + GELU_C3 * (x * x))
    return x / (1.0 + jnp.exp(-y2))


def _silu(x):
    return x * jax.nn.sigmoid(x)


def _softplus(x):
    return jnp.maximum(x, 0.0) + jnp.log1p(jnp.exp(-jnp.abs(x)))


def _log_sigmoid(x):
    return -_softplus(-x)


def _dot(a, b):
    return jnp.dot(a, b, preferred_element_type=F32)


def _dot_nt(a, b):
    return lax.dot_general(a, b, (((1,), (1,)), ((), ())), preferred_element_type=F32)


def _shift_rows(x, k, carry):
    rolled = pltpu.roll(x, k, 0)
    croll = pltpu.roll(carry, k, 0)
    rows = lax.broadcasted_iota(jnp.int32, carry.shape, 0)
    top = jnp.where(rows < k, croll, rolled[:SUBLANES])
    return jnp.concatenate([top, rolled[SUBLANES:]], axis=0)


def _causal_conv(x, carry, w_ref, b_ref):
    width = w_ref.shape[0]
    y = b_ref[...] + x * w_ref[width - 1:width, :]
    for k in range(1, width):
        y = y + _shift_rows(x, k, carry) * w_ref[width - 1 - k:width - k, :]
    return y


def _cumsum_rows(x):
    n = x.shape[0]
    rows = lax.broadcasted_iota(jnp.int32, x.shape, 0)
    s = 1
    while s < n:
        x = x + jnp.where(rows >= s, pltpu.roll(x, s, 0), 0.0)
        s *= 2
    return x


def _expand_heads(x, e_ref):
    e = e_ref[...]
    x1 = x.astype(BF16)
    r1 = x - x1.astype(F32)
    x2 = r1.astype(BF16)
    x3 = (r1 - x2.astype(F32)).astype(BF16)
    return _dot(x1, e) + _dot(x2, e) + _dot(x3, e)


def _rmsnorm(x, g):
    ms = jnp.mean(x * x, axis=-1, keepdims=True)
    return x * lax.rsqrt(ms + EPS) * g


def _rms_matmul_kernel(*refs, n_out, has_side):
    x_ref, g_ref, w_ref = refs[:3]
    pos = 3
    ws_ref = None
    if has_side:
        ws_ref = refs[pos]
        pos += 1
    out_refs = refs[pos:pos + n_out]
    pos += n_out
    side_ref = None
    if has_side:
        side_ref = refs[pos]
        pos += 1
    hn_ref = refs[pos]

    @pl.when(pl.program_id(1) == 0)
    def _():
        hn = _rmsnorm(x_ref[...], g_ref[...]).astype(BF16)
        hn_ref[...] = hn
        if has_side:
            side_ref[...] = _dot(hn, ws_ref[...])

    acc = _dot(hn_ref[...], w_ref[...])
    for o in out_refs:
        o[...] = acc.astype(o.dtype)


def _rms_matmul(x, g, w, out_dtypes, col0, n, w_side=None, tm=1024, tn=1024):
    m, d = x.shape
    tm = min(tm, m)
    tn = min(tn, n)
    assert m % tm == 0 and n % tn == 0 and col0 % tn == 0
    jb = col0 // tn
    has_side = w_side is not None
    in_specs = [
        pl.BlockSpec((tm, d), lambda i, j: (i, 0)),
        pl.BlockSpec((1, d), lambda i, j: (0, 0)),
        pl.BlockSpec((d, tn), lambda i, j: (0, jb + j)),
    ]
    args = [x, g.reshape(1, d), w]
    out_shape = [jax.ShapeDtypeStruct((m, n), dt) for dt in out_dtypes]
    out_specs = [pl.BlockSpec((tm, tn), lambda i, j: (i, j)) for _ in out_dtypes]
    if has_side:
        ns = w_side.shape[1]
        in_specs.append(pl.BlockSpec((d, ns), lambda i, j: (0, 0)))
        args.append(w_side)
        out_shape.append(jax.ShapeDtypeStruct((m, ns), F32))
        out_specs.append(pl.BlockSpec((tm, ns), lambda i, j: (i, 0)))
    outs = pl.pallas_call(
        functools.partial(_rms_matmul_kernel, n_out=len(out_dtypes), has_side=has_side),
        grid=(m // tm, n // tn),
        in_specs=in_specs,
        out_specs=out_specs,
        out_shape=out_shape,
        scratch_shapes=[pltpu.VMEM((tm, d), BF16)],
        compiler_params=_cparams("parallel", "arbitrary"),
        name="rms_matmul",
    )(*args)
    return outs


def _kv_proj_kernel(x_ref, g_ref, w_ref, of_ref, ob_ref):
    acc = _dot(_rmsnorm(x_ref[...], g_ref[...]).astype(BF16), w_ref[...])
    ob_ref[...] = acc.astype(BF16)
    tm = acc.shape[0]
    for h in range(D_HEADS):
        of_ref[pl.ds(h, tm, stride=D_HEADS), :] = acc[:, h * D_HDIM:(h + 1) * D_HDIM]


def _kv_proj(x, g, w, col0, tm=512):
    m, d = x.shape
    n = D_HEADS * D_HDIM
    tm = min(tm, m)
    assert m % tm == 0 and col0 % n == 0
    jb = col0 // n
    return pl.pallas_call(
        _kv_proj_kernel,
        grid=(m // tm,),
        in_specs=[
            pl.BlockSpec((tm, d), lambda i: (i, 0)),
            pl.BlockSpec((1, d), lambda i: (0, 0)),
            pl.BlockSpec((d, n), lambda i: (0, jb), pipeline_mode=pl.Buffered(1)),
        ],
        out_specs=[
            pl.BlockSpec((tm * D_HEADS, D_HDIM), lambda i: (i, 0)),
            pl.BlockSpec((tm, n), lambda i: (i, 0)),
        ],
        out_shape=[
            jax.ShapeDtypeStruct((m * D_HEADS, D_HDIM), F32),
            jax.ShapeDtypeStruct((m, n), BF16),
        ],
        compiler_params=_cparams("parallel"),
        name="kv_proj",
    )(x, g.reshape(1, d), w)


def _matmul_res_kernel(a1_ref, a2_ref, w1_ref, w2_ref, r_ref, o_ref):
    o_ref[...] = r_ref[...] + _dot(a1_ref[...], w1_ref[...]) + _dot(a2_ref[...], w2_ref[...])


def _matmul_res(a1, a2, w, res, tm=512):
    m, k1 = a1.shape
    k2 = a2.shape[1]
    n = w.shape[1]
    tm = min(tm, m)
    assert m % tm == 0 and k1 == k2
    once = pl.Buffered(1)
    return pl.pallas_call(
        _matmul_res_kernel,
        grid=(m // tm,),
        in_specs=[
            pl.BlockSpec((tm, k1), lambda i: (i, 0)),
            pl.BlockSpec((tm, k2), lambda i: (i, 0)),
            pl.BlockSpec((k1, n), lambda i: (0, 0), pipeline_mode=once),
            pl.BlockSpec((k2, n), lambda i: (1, 0), pipeline_mode=once),
            pl.BlockSpec((tm, n), lambda i: (i, 0)),
        ],
        out_specs=pl.BlockSpec((tm, n), lambda i: (i, 0)),
        out_shape=jax.ShapeDtypeStruct((m, n), F32),
        compiler_params=_cparams("parallel"),
        name="matmul_res",
    )(a1, a2, w, w, res)


def _ffn_prompt_kernel(x_ref, g_ref, wg_ref, wu_ref, cw_ref, cb_ref, wd_ref, gf_ref,
                       o_ref, tail_ref, hn_ref, carry_ref, *, tiles_per_seq, final_norm):
    i = pl.program_id(0)
    j = pl.program_id(1)
    nj = pl.num_programs(1)

    @pl.when(j == 0)
    def _():
        x = x_ref[...]
        hn_ref[...] = _rmsnorm(x, g_ref[...]).astype(BF16)
        o_ref[...] = x

    @pl.when(i % tiles_per_seq == 0)
    def _():
        carry_ref[j] = jnp.zeros(carry_ref.shape[1:], F32)

    hn = hn_ref[...]
    graw = _dot(hn, wg_ref[...])
    up = _dot(hn, wu_ref[...])
    gc = _causal_conv(graw, carry_ref[j], cw_ref, cb_ref)
    tail = graw[graw.shape[0] - SUBLANES:, :]
    carry_ref[j] = tail
    tail_ref[0] = tail
    act = (_silu(gc) * up).astype(BF16)
    o_ref[...] += _dot(act, wd_ref[...])

    if final_norm:
        @pl.when(j == nj - 1)
        def _():
            o_ref[...] = _rmsnorm(o_ref[...], gf_ref[...])


def _ffn_prompt(x, g, wg, wu, cw, cb, wd, gf, layer, seq, final_norm, tm=1024, tf=512):
    m, d = x.shape
    ff = wg.shape[2]
    tm = min(tm, seq)
    assert m % tm == 0 and ff % tf == 0 and seq % tm == 0
    tiles_per_seq = seq // tm
    out, tails = pl.pallas_call(
        functools.partial(_ffn_prompt_kernel, tiles_per_seq=tiles_per_seq, final_norm=final_norm),
        grid=(m // tm, ff // tf),
        in_specs=[
            pl.BlockSpec((tm, d), lambda i, j: (i, 0)),
            pl.BlockSpec((1, d), lambda i, j: (0, 0)),
            pl.BlockSpec((None, d, tf), lambda i, j: (layer, 0, j)),
            pl.BlockSpec((None, d, tf), lambda i, j: (layer, 0, j)),
            pl.BlockSpec((cw.shape[0], tf), lambda i, j: (0, j)),
            pl.BlockSpec((1, tf), lambda i, j: (0, j)),
            pl.BlockSpec((None, tf, d), lambda i, j: (layer, j, 0)),
            pl.BlockSpec((1, d), lambda i, j: (0, 0)),
        ],
        out_specs=[
            pl.BlockSpec((tm, d), lambda i, j: (i, 0)),
            pl.BlockSpec((1, SUBLANES, tf), lambda i, j: (i, 0, j)),
        ],
        out_shape=[
            jax.ShapeDtypeStruct((m, d), F32),
            jax.ShapeDtypeStruct((m // tm, SUBLANES, ff), F32),
        ],
        scratch_shapes=[
            pltpu.VMEM((tm, d), BF16),
            pltpu.VMEM((ff // tf, SUBLANES, tf), F32),
        ],
        compiler_params=_cparams("arbitrary", "arbitrary"),
        name="ffn_prompt",
    )(x, g.reshape(1, d), wg, wu, cw, cb.reshape(1, ff), wd, gf.reshape(1, d))
    return out, tails[tiles_per_seq - 1::tiles_per_seq]


def _ffn_sample_kernel(x_ref, g_ref, wg_ref, wu_ref, cw_ref, cb_ref, pre_ref, wd_ref, gf_ref,
                       o_ref, npre_ref, hn_ref, acc_ref, *, final_norm):
    j = pl.program_id(0)
    nj = pl.num_programs(0)

    @pl.when(j == 0)
    def _():
        hn_ref[...] = _rmsnorm(x_ref[...], g_ref[...]).astype(BF16)
        acc_ref[...] = jnp.zeros_like(acc_ref)

    hn = hn_ref[...]
    graw = _dot(hn, wg_ref[...])
    up = _dot(hn, wu_ref[...])
    p0 = pre_ref[0]
    p1 = pre_ref[1]
    gc = cb_ref[...] + graw * cw_ref[2:3, :] + p1 * cw_ref[1:2, :] + p0 * cw_ref[0:1, :]
    npre_ref[0] = p1
    npre_ref[1] = graw
    act = (_silu(gc) * up).astype(BF16)
    acc_ref[...] += _dot(act, wd_ref[...])

    @pl.when(j == nj - 1)
    def _():
        y = x_ref[...] + acc_ref[...]
        if final_norm:
            y = _rmsnorm(y, gf_ref[...])
        o_ref[...] = y


def _ffn_sample(x, g, wg, wu, cw, cb, prefix, wd, gf, layer, final_norm, tf=512):
    m, d = x.shape
    ff = wg.shape[2]
    npre = prefix.shape[0]
    out, new_prefix = pl.pallas_call(
        functools.partial(_ffn_sample_kernel, final_norm=final_norm),
        grid=(ff // tf,),
        in_specs=[
            pl.BlockSpec((m, d), lambda j: (0, 0)),
            pl.BlockSpec((1, d), lambda j: (0, 0)),
            pl.BlockSpec((None, d, tf), lambda j: (layer, 0, j)),
            pl.BlockSpec((None, d, tf), lambda j: (layer, 0, j)),
            pl.BlockSpec((cw.shape[0], tf), lambda j: (0, j)),
            pl.BlockSpec((1, tf), lambda j: (0, j)),
            pl.BlockSpec((npre, m, tf), lambda j: (0, 0, j)),
            pl.BlockSpec((None, tf, d), lambda j: (layer, j, 0)),
            pl.BlockSpec((1, d), lambda j: (0, 0)),
        ],
        out_specs=[
            pl.BlockSpec((m, d), lambda j: (0, 0)),
            pl.BlockSpec((npre, m, tf), lambda j: (0, 0, j)),
        ],
        out_shape=[
            jax.ShapeDtypeStruct((m, d), F32),
            jax.ShapeDtypeStruct((npre, m, ff), F32),
        ],
        scratch_shapes=[pltpu.VMEM((m, d), BF16), pltpu.VMEM((m, d), F32)],
        compiler_params=_cparams("arbitrary"),
        name="ffn_sample",
    )(x, g.reshape(1, d), wg, wu, cw, cb.reshape(1, ff), prefix, wd, gf.reshape(1, d))
    return out, new_prefix


def _even_prompt_kernel(au_ref, av_ref, z_ref, xbc_ref, dtr_ref,
                        lng_ref, lnb_ref, sw_ref, sbt_ref, cw_ref, cb_ref, dtb_ref, alog_ref,
                        dsk_ref, ng_ref, e_ref,
                        ya_ref, yb_ref, tail_ref, st_ref,
                        ht_ref, carry_ref):
    c = pl.program_id(1)
    nc = pl.num_programs(1)

    @pl.when(c == 0)
    def _():
        ht_ref[...] = jnp.zeros_like(ht_ref)
        carry_ref[...] = jnp.zeros_like(carry_ref)

    rows = lax.broadcasted_iota(jnp.int32, (CHUNK, CHUNK), 0)
    cols = lax.broadcasted_iota(jnp.int32, (CHUNK, CHUNK), 1)
    tri = rows >= cols

    u = _gelu(au_ref[...].astype(F32))
    gv = _gelu(av_ref[...].astype(F32))
    mu = jnp.mean(gv, axis=-1, keepdims=True)
    xc = gv - mu
    var = jnp.mean(xc * xc, axis=-1, keepdims=True)
    v = xc * lax.rsqrt(var + EPS) * lng_ref[...] + lnb_ref[...]
    vb = v.astype(BF16)
    parts = []
    for g in range(A_GROUPS):
        wg = jnp.where(tri, sw_ref[g], 0.0).astype(BF16)
        parts.append(_dot(wg, vb[:, g * A_GDIM:(g + 1) * A_GDIM]) + sbt_ref[:, g:g + 1])
    ya_ref[...] = (u * jnp.concatenate(parts, axis=1)).astype(BF16)

    xr = xbc_ref[...].astype(F32)
    xbc = _silu(_causal_conv(xr, carry_ref[...], cw_ref, cb_ref))
    carry_ref[...] = xr[CHUNK - SUBLANES:, :]
    xs = xbc[:, :D_MODEL]
    bm = xbc[:, D_MODEL:D_MODEL + B_GROUPS * B_STATE]
    cm = xbc[:, D_MODEL + B_GROUPS * B_STATE:]

    dt = _softplus(dtr_ref[...] + dtb_ref[...])
    a = -jnp.exp(alog_ref[...])
    acs = _cumsum_rows(dt * a)
    ea = jnp.exp(acs)
    te = jnp.exp(acs[CHUNK - 1:CHUNK, :] - acs) * dt
    ea_x = _expand_heads(ea, e_ref)
    te_x = _expand_heads(te, e_ref)
    acs_t = acs.T
    dt_t = dt.T

    lane = lax.broadcasted_iota(jnp.int32, xs.shape, 1)
    lo = (lane & (2 * B_HDIM - 1)) < B_HDIM
    xs_half = (jnp.where(lo, xs, 0.0).astype(BF16), jnp.where(lo, 0.0, xs).astype(BF16))
    xsc_b = (xs * te_x).astype(BF16)
    ht_old = ht_ref[...]
    ht_b = ht_old.astype(BF16)

    ys = []
    states = []
    for g in range(B_GROUPS):
        gsl = slice(g * B_GW, (g + 1) * B_GW)
        bg = bm[:, g * B_STATE:(g + 1) * B_STATE]
        cg_b = cm[:, g * B_STATE:(g + 1) * B_STATE].astype(BF16)
        cb = _dot_nt(cg_b, bg.astype(BF16))
        y_off = _dot(cg_b, ht_b[:, gsl]) * ea_x[:, gsl]
        states.append(_dot(bg.T.astype(BF16), xsc_b[:, gsl]))
        pair_out = []
        for jp in range(B_GW // (2 * B_HDIM)):
            h0 = g * (B_GW // B_HDIM) + 2 * jp
            acc = None
            for e in range(2):
                h = h0 + e
                seg = acs[:, h:h + 1] - acs_t[h:h + 1, :]
                dec = jnp.exp(jnp.where(tri, seg, -jnp.inf))
                mh = (cb * dec * dt_t[h:h + 1, :]).astype(BF16)
                r = _dot(mh, xs_half[e][:, h0 * B_HDIM:(h0 + 2) * B_HDIM])
                acc = r if acc is None else acc + r
            pair_out.append(acc)
        ys.append(jnp.concatenate(pair_out, axis=1) + y_off)
    y = jnp.concatenate(ys, axis=1)
    ht_ref[...] = ea_x[CHUNK - 1:CHUNK, :] * ht_old + jnp.concatenate(states, axis=1)

    y = (y + dsk_ref[...] * xs) * _silu(z_ref[...].astype(F32))
    outs = []
    for g in range(B_GROUPS):
        yg = y[:, g * B_GW:(g + 1) * B_GW]
        ms = jnp.mean(yg * yg, axis=-1, keepdims=True)
        outs.append(yg * lax.rsqrt(ms + EPS))
    yb_ref[...] = (jnp.concatenate(outs, axis=1) * ng_ref[...]).astype(BF16)

    @pl.when(c == nc - 1)
    def _():
        tail_ref[0] = xr[CHUNK - SUBLANES:, :]
        st_ref[0] = ht_ref[...].T


def _even_prompt(proj, dtraw, nb, seq, p):
    n = proj.shape[0]
    nc = seq // CHUNK
    d = D_MODEL

    def row(b, c):
        return b * nc + c

    full = lambda shape: pl.BlockSpec(shape, lambda b, c: (0,) * len(shape))
    ya, yb, tails, state = pl.pallas_call(
        _even_prompt_kernel,
        grid=(nb, nc),
        in_specs=[
            pl.BlockSpec((CHUNK, d), lambda b, c: (row(b, c), 0)),
            pl.BlockSpec((CHUNK, d), lambda b, c: (row(b, c), 1)),
            pl.BlockSpec((CHUNK, d), lambda b, c: (row(b, c), 2)),
            pl.BlockSpec((CHUNK, B_CONV_DIM), lambda b, c: (row(b, c), 2)),
            pl.BlockSpec((CHUNK, LANES), lambda b, c: (row(b, c), 0)),
            full((1, d)), full((1, d)), full((A_GROUPS, CHUNK, CHUNK)), full((CHUNK, A_GROUPS)),
            full((4, B_CONV_DIM)), full((1, B_CONV_DIM)), full((1, LANES)), full((1, LANES)),
            full((1, d)), full((1, d)), full((LANES, d)),
        ],
        out_specs=[
            pl.BlockSpec((CHUNK, d), lambda b, c: (row(b, c), 0)),
            pl.BlockSpec((CHUNK, d), lambda b, c: (row(b, c), 0)),
            pl.BlockSpec((1, SUBLANES, B_CONV_DIM), lambda b, c: (b, 0, 0)),
            pl.BlockSpec((1, d, B_STATE), lambda b, c: (b, 0, 0)),
        ],
        out_shape=[
            jax.ShapeDtypeStruct((n, d), BF16),
            jax.ShapeDtypeStruct((n, d), BF16),
            jax.ShapeDtypeStruct((nb, SUBLANES, B_CONV_DIM), F32),
            jax.ShapeDtypeStruct((nb, d, B_STATE), F32),
        ],
        scratch_shapes=[pltpu.VMEM((B_STATE, d), F32), pltpu.VMEM((SUBLANES, B_CONV_DIM), F32)],
        compiler_params=_cparams("parallel", "arbitrary"),
        name="even_prompt",
    )(proj, proj, proj, proj, dtraw,
      p["ln_g"], p["ln_b"], p["sgu_w"], p["sgu_bt"], p["conv_w"], p["conv_b"], p["dt_bias"], p["a_log"],
      p["d_skip"], p["norm_g"], p["expand"])
    return ya, yb, tails, state


def _even_sample_a_kernel(proj_ref, dtr_ref, pre_ref, lng_ref, lnb_ref, w0_ref, b0_ref, cw_ref, cb_ref,
                          dtb_ref, alogx_ref, e_ref,
                          ya_ref, v_ref, npre_ref, xs_ref, bm_ref, cm_ref, dtx_ref, dec_ref):
    d = D_MODEL
    u = _gelu(proj_ref[:, 0:d])
    gv = _gelu(proj_ref[:, d:2 * d])
    mu = jnp.mean(gv, axis=-1, keepdims=True)
    xc = gv - mu
    var = jnp.mean(xc * xc, axis=-1, keepdims=True)
    v = xc * lax.rsqrt(var + EPS) * lng_ref[...] + lnb_ref[...]
    v_ref[...] = v
    ya_ref[...] = (u * (w0_ref[...] * v + b0_ref[...])).astype(BF16)

    xr = proj_ref[:, 3 * d:3 * d + B_CONV_DIM]
    p0, p1, p2 = pre_ref[0], pre_ref[1], pre_ref[2]
    y = (cb_ref[...] + xr * cw_ref[3:4, :] + p2 * cw_ref[2:3, :] + p1 * cw_ref[1:2, :]
         + p0 * cw_ref[0:1, :])
    npre_ref[0] = p1
    npre_ref[1] = p2
    npre_ref[2] = xr
    xbc = _silu(y)
    xs = xbc[:, :d]
    xs_ref[...] = xs
    bm_ref[...] = xbc[:, d:d + B_GROUPS * B_STATE]
    cm_ref[...] = xbc[:, d + B_GROUPS * B_STATE:]
    dt_x = _expand_heads(_softplus(dtr_ref[...] + dtb_ref[...]), e_ref)
    dtx_ref[...] = dt_x * xs
    dec_ref[...] = jnp.exp(dt_x * (-jnp.exp(alogx_ref[...])))


def _even_sample_a(proj, dtraw, prefix, p):
    m = proj.shape[0]
    d = D_MODEL
    gs = B_GROUPS * B_STATE
    return pl.pallas_call(
        _even_sample_a_kernel,
        out_shape=[
            jax.ShapeDtypeStruct((m, d), BF16),
            jax.ShapeDtypeStruct((m, d), F32),
            jax.ShapeDtypeStruct(prefix.shape, F32),
            jax.ShapeDtypeStruct((m, d), F32),
            jax.ShapeDtypeStruct((m, gs), F32),
            jax.ShapeDtypeStruct((m, gs), F32),
            jax.ShapeDtypeStruct((m, d), F32),
            jax.ShapeDtypeStruct((m, d), F32),
        ],
        compiler_params=pltpu.CompilerParams(vmem_limit_bytes=VMEM_LIMIT),
        name="even_sample_a",
    )(proj, dtraw, prefix, p["ln_g"], p["ln_b"], p["sgu_w0"], p["sgu_b0"], p["conv_w"], p["conv_b"],
      p["dt_bias"], p["a_log_x"], p["expand"])


def _ssm_state_sample_kernel(h_ref, dtx_ref, dec_ref, bm_ref, cm_ref, hn_ref, y_ref):
    ns = h_ref.shape[2]
    for g in range(B_GROUPS):
        sl = slice(g * B_GW, (g + 1) * B_GW)
        dec_col = jnp.broadcast_to(dec_ref[0, :, sl], (ns, B_GW)).T
        dtx_col = jnp.broadcast_to(dtx_ref[0, :, sl], (ns, B_GW)).T
        hn = dec_col * h_ref[0, sl, :] + dtx_col * bm_ref[0, g:g + 1, :]
        hn_ref[0, sl, :] = hn
        c_rows = jnp.broadcast_to(cm_ref[0, g:g + 1, :], (SUBLANES, ns)).astype(BF16)
        y_ref[0, :, sl] = _dot_nt(c_rows, hn.astype(BF16))[0:1, :]


def _ssm_state_sample(h0, dtx, dec, bm, cm):
    m, d, ns = h0.shape
    row = pl.BlockSpec((1, 1, d), lambda b: (b, 0, 0))
    grp = pl.BlockSpec((1, B_GROUPS, ns), lambda b: (b, 0, 0))
    st = pl.BlockSpec((1, d, ns), lambda b: (b, 0, 0))
    return pl.pallas_call(
        _ssm_state_sample_kernel,
        grid=(m,),
        in_specs=[st, row, row, grp, grp],
        out_specs=[st, row],
        out_shape=[jax.ShapeDtypeStruct((m, d, ns), F32), jax.ShapeDtypeStruct((m, 1, d), F32)],
        compiler_params=_cparams("parallel"),
        name="ssm_state_sample",
    )(h0, dtx, dec, bm, cm)


def _even_sample_c_kernel(y_ref, xs_ref, z_ref, dsk_ref, ng_ref, yb_ref):
    y = (y_ref[...] + dsk_ref[...] * xs_ref[...]) * _silu(z_ref[...])
    outs = []
    for g in range(B_GROUPS):
        yg = y[:, g * B_GW:(g + 1) * B_GW]
        ms = jnp.mean(yg * yg, axis=-1, keepdims=True)
        outs.append(yg * lax.rsqrt(ms + EPS))
    yb_ref[...] = (jnp.concatenate(outs, axis=1) * ng_ref[...]).astype(BF16)


def _even_sample_c(y, xs, z, p):
    return pl.pallas_call(
        _even_sample_c_kernel,
        out_shape=jax.ShapeDtypeStruct(y.shape, BF16),
        name="even_sample_c",
    )(y, xs, z, p["d_skip"], p["norm_g"])


def _lru_gates(xc, wa_ref, ba_ref, wx_ref, bx_ref, lam_ref):
    xcb = xc.astype(BF16)
    rp, ip = [], []
    for h in range(C_HEADS):
        xh = xcb[:, h * C_BLOCK:(h + 1) * C_BLOCK]
        rp.append(_dot(xh, wa_ref[h]))
        ip.append(_dot(xh, wx_ref[h]))
    r = jax.nn.sigmoid(jnp.concatenate(rp, axis=1) + ba_ref[...])
    ig = jax.nn.sigmoid(jnp.concatenate(ip, axis=1) + bx_ref[...])
    log_a = -C_GATE * r * _softplus(-lam_ref[...])
    a = jnp.exp(log_a)
    bt = jnp.sqrt(-jnp.tanh(log_a) * (a * a + 1.0)) * (ig * xc)
    return a, bt


def _lru_prompt_kernel(gy_ref, gx_ref, fr_ref, cw_ref, cb_ref, wa_ref, ba_ref, wx_ref, bx_ref, lam_ref, bf_ref,
                       yc_ref, lf_ref, ccol_ref, crow_ref, last_ref, tail_ref,
                       a_s, b_s, h_s, hc_ref, cc_ref, fc_ref):
    t = pl.program_id(1)
    nt = pl.num_programs(1)
    tm = gy_ref.shape[0]

    @pl.when(t == 0)
    def _():
        hc_ref[...] = jnp.zeros_like(hc_ref)
        cc_ref[...] = jnp.zeros_like(cc_ref)
        fc_ref[...] = jnp.zeros_like(fc_ref)

    xr = gx_ref[...].astype(F32)
    xc = _causal_conv(xr, cc_ref[...], cw_ref, cb_ref)
    cc_ref[...] = xr[tm - SUBLANES:, :]
    a, bt = _lru_gates(xc, wa_ref, ba_ref, wx_ref, bx_ref, lam_ref)
    a_s[...] = a
    b_s[...] = bt

    def body(i, h):
        h = a_s[pl.ds(i, 1), :] * h + b_s[pl.ds(i, 1), :]
        h_s[pl.ds(i, 1), :] = h
        return h

    h = lax.fori_loop(0, tm, body, hc_ref[0:1, :], unroll=8)
    hc_ref[0:1, :] = h
    yc_ref[...] = (h_s[...] * _gelu(gy_ref[...].astype(F32))).astype(BF16)

    lf = _log_sigmoid(fr_ref[...] + bf_ref[...])
    cs = _cumsum_rows(lf) + fc_ref[0:1, :]
    fc_ref[0:1, :] = cs[tm - 1:tm, :]
    lf_ref[...] = lf
    ccol_ref[...] = cs
    crow_ref[0] = cs.T[:D_HEADS, :]

    @pl.when(t == nt - 1)
    def _():
        last_ref[0] = h
        tail_ref[0] = xr[tm - SUBLANES:, :]


def _lru_prompt(proj, fraw, nb, seq, p, tm=256):
    n = proj.shape[0]
    d = D_MODEL
    nt = seq // tm

    def row(b, t):
        return b * nt + t

    full = lambda shape: pl.BlockSpec(shape, lambda b, t: (0,) * len(shape))
    return pl.pallas_call(
        _lru_prompt_kernel,
        grid=(nb, nt),
        in_specs=[
            pl.BlockSpec((tm, d), lambda b, t: (row(b, t), 0)),
            pl.BlockSpec((tm, d), lambda b, t: (row(b, t), 1)),
            pl.BlockSpec((tm, LANES), lambda b, t: (row(b, t), 0)),
            full((4, d)), full((1, d)),
            full((C_HEADS, C_BLOCK, C_BLOCK)), full((1, d)),
            full((C_HEADS, C_BLOCK, C_BLOCK)), full((1, d)),
            full((1, d)), full((1, LANES)),
        ],
        out_specs=[
            pl.BlockSpec((tm, d), lambda b, t: (row(b, t), 0)),
            pl.BlockSpec((tm, LANES), lambda b, t: (row(b, t), 0)),
            pl.BlockSpec((tm, LANES), lambda b, t: (row(b, t), 0)),
            pl.BlockSpec((1, D_HEADS, tm), lambda b, t: (b, 0, t)),
            pl.BlockSpec((1, 1, d), lambda b, t: (b, 0, 0)),
            pl.BlockSpec((1, SUBLANES, d), lambda b, t: (b, 0, 0)),
        ],
        out_shape=[
            jax.ShapeDtypeStruct((n, d), BF16),
            jax.ShapeDtypeStruct((n, LANES), F32),
            jax.ShapeDtypeStruct((n, LANES), F32),
            jax.ShapeDtypeStruct((nb, D_HEADS, seq), F32),
            jax.ShapeDtypeStruct((nb, 1, d), F32),
            jax.ShapeDtypeStruct((nb, SUBLANES, d), F32),
        ],
        scratch_shapes=[
            pltpu.VMEM((tm, d), F32), pltpu.VMEM((tm, d), F32), pltpu.VMEM((tm, d), F32),
            pltpu.VMEM((SUBLANES, d), F32), pltpu.VMEM((SUBLANES, d), F32), pltpu.VMEM((SUBLANES, LANES), F32),
        ],
        compiler_params=_cparams("parallel", "arbitrary"),
        name="lru_prompt",
    )(proj, proj, fraw, p["conv_w"], p["conv_b"], p["w_a"], p["b_a"], p["w_x"], p["b_x"], p["lam"], p["b_f"])


def _attn_prompt_kernel(q_ref, k_ref, v_ref, cq_ref, ck_ref, o_ref, m_ref, l_ref, acc_ref, cqb_ref):
    qi = pl.program_id(1)
    ki = pl.program_id(2)
    tq = q_ref.shape[0]
    tk = k_ref.shape[0]
    rep = tk // LANES
    scale2 = (D_HDIM ** -0.5) * LOG2E

    @pl.when(ki == 0)
    def _():
        m_ref[...] = jnp.full_like(m_ref, NEG)
        l_ref[...] = jnp.zeros_like(l_ref)
        acc_ref[...] = jnp.zeros_like(acc_ref)
        cq2 = cq_ref[...] * LOG2E
        for h in range(D_HEADS):
            cqb_ref[h] = jnp.broadcast_to(cq2[:, h:h + 1], (tq, LANES))

    def tile(on_diagonal):
        if on_diagonal:
            causal = (lax.broadcasted_iota(jnp.int32, (tq, tk), 0)
                      >= lax.broadcasted_iota(jnp.int32, (tq, tk), 1))
        ck2 = ck_ref[0] * LOG2E
        for h in range(D_HEADS):
            hs = slice(h * D_HDIM, (h + 1) * D_HDIM)
            t = _dot_nt(q_ref[:, hs], k_ref[:, hs]) * scale2 - ck2[h:h + 1, :]
            if on_diagonal:
                t = jnp.where(causal, t, NEG)
            cqh = cqb_ref[h]
            m_old = m_ref[h]
            rmax = jnp.broadcast_to(jnp.max(t, axis=-1, keepdims=True), (tq, LANES))
            m_new = jnp.maximum(m_old, rmax + cqh)
            alpha = jnp.exp2(m_old - m_new)
            pm = jnp.exp2(t - jnp.concatenate([m_new - cqh] * rep, axis=1))
            rsum = jnp.broadcast_to(jnp.sum(pm, axis=-1, keepdims=True), (tq, LANES))
            l_ref[h] = alpha * l_ref[h] + rsum
            acc_ref[:, hs] = alpha * acc_ref[:, hs] + _dot(pm.astype(BF16), v_ref[:, hs])
            m_ref[h] = m_new

    @pl.when(ki < qi)
    def _():
        tile(False)

    @pl.when(ki == qi)
    def _():
        tile(True)
        for h in range(D_HEADS):
            hs = slice(h * D_HDIM, (h + 1) * D_HDIM)
            o_ref[:, hs] = (acc_ref[:, hs] * (1.0 / l_ref[h])).astype(BF16)


def _attn_prompt(proj, kb, vb, ccol, crow, nb, seq, tq=512):
    n = proj.shape[0]
    d = D_MODEL
    tq = min(tq, seq)
    nq = seq // tq
    return pl.pallas_call(
        _attn_prompt_kernel,
        grid=(nb, nq, nq),
        in_specs=[
            pl.BlockSpec((tq, d), lambda b, qi, ki: (b * nq + qi, 2)),
            pl.BlockSpec((tq, d), lambda b, qi, ki: (b * nq + jnp.minimum(ki, qi), 0)),
            pl.BlockSpec((tq, d), lambda b, qi, ki: (b * nq + jnp.minimum(ki, qi), 0)),
            pl.BlockSpec((tq, LANES), lambda b, qi, ki: (b * nq + qi, 0)),
            pl.BlockSpec((1, D_HEADS, tq), lambda b, qi, ki: (b, 0, jnp.minimum(ki, qi))),
        ],
        out_specs=pl.BlockSpec((tq, d), lambda b, qi, ki: (b * nq + qi, 0)),
        out_shape=jax.ShapeDtypeStruct((n, d), BF16),
        scratch_shapes=[pltpu.VMEM((D_HEADS, tq, LANES), F32), pltpu.VMEM((D_HEADS, tq, LANES), F32),
                        pltpu.VMEM((tq, d), F32), pltpu.VMEM((D_HEADS, tq, LANES), F32)],
        compiler_params=_cparams("parallel", "parallel", "arbitrary"),
        name="attn_prompt",
    )(proj, kb, vb, ccol, crow)


def _odd_sample_kernel(gy_ref, gx_ref, fr_ref, pre_ref, h0_ref, cw_ref, cb_ref, wa_ref, ba_ref, wx_ref, bx_ref,
                       lam_ref, bf_ref, yc_ref, hn_ref, npre_ref, lf_ref):
    xr = gx_ref[...]
    p0, p1, p2 = pre_ref[0], pre_ref[1], pre_ref[2]
    xc = (cb_ref[...] + xr * cw_ref[3:4, :] + p2 * cw_ref[2:3, :] + p1 * cw_ref[1:2, :]
          + p0 * cw_ref[0:1, :])
    npre_ref[0] = p1
    npre_ref[1] = p2
    npre_ref[2] = xr
    a, bt = _lru_gates(xc, wa_ref, ba_ref, wx_ref, bx_ref, lam_ref)
    hn = a * h0_ref[...] + bt
    hn_ref[...] = hn
    yc_ref[...] = (hn * _gelu(gy_ref[...])).astype(BF16)
    lf_ref[...] = _log_sigmoid(fr_ref[...] + bf_ref[...])


def _odd_sample(gy, gx, fraw, prefix, h0, p):
    m, d = gy.shape
    return pl.pallas_call(
        _odd_sample_kernel,
        out_shape=[
            jax.ShapeDtypeStruct((m, d), BF16),
            jax.ShapeDtypeStruct((m, d), F32),
            jax.ShapeDtypeStruct(prefix.shape, F32),
            jax.ShapeDtypeStruct((m, LANES), F32),
        ],
        compiler_params=pltpu.CompilerParams(vmem_limit_bytes=VMEM_LIMIT),
        name="odd_sample",
    )(gy, gx, fraw, prefix, h0, p["conv_w"], p["conv_b"], p["w_a"], p["b_a"], p["w_x"], p["b_x"], p["lam"],
      p["b_f"])


def _page_bias_kernel(lf_ref, ex_ref, tot_ref):
    x0 = lf_ref[...]
    n = x0.shape[1]
    lane = lax.broadcasted_iota(jnp.int32, x0.shape, 1)
    ex = jnp.where(lane + D_HEADS < n, pltpu.roll(x0, n - D_HEADS, 1), 0.0)
    tot = x0
    s = D_HEADS
    while s < n:
        ex = ex + jnp.where(lane + s < n, pltpu.roll(ex, n - s, 1), 0.0)
        tot = tot + pltpu.roll(tot, n - s, 1)
        s *= 2
    ex_ref[...] = ex
    tot_ref[...] = tot


def _page_bias(lf_flat, tr=256):
    r, n = lf_flat.shape
    tr = math.gcd(r, tr)
    spec = pl.BlockSpec((tr, n), lambda i: (i, 0))
    return pl.pallas_call(
        _page_bias_kernel,
        grid=(r // tr,),
        in_specs=[spec],
        out_specs=[spec, spec],
        out_shape=[jax.ShapeDtypeStruct((r, n), F32), jax.ShapeDtypeStruct((r, n), F32)],
        compiler_params=_cparams("parallel"),
        name="page_bias",
    )(lf_flat)


def _paged_attn_kernel(pt_ref, q_ref, kn_ref, vn_ref, lfn_ref, *rest, pages):
    k_refs = rest[0:pages]
    v_refs = rest[pages:2 * pages]
    ex_refs = rest[2 * pages:3 * pages]
    tot_refs = rest[3 * pages:4 * pages]
    o_ref, m_ref, l_ref, acc_ref, carry_ref = rest[4 * pages:]
    p = pl.program_id(1)
    npg = pl.num_programs(1)
    scale = D_HDIM ** -0.5
    q = q_ref[0]

    @pl.when(p == 0)
    def _():
        s0 = jnp.sum(q * kn_ref[0], axis=-1, keepdims=True) * scale
        m_ref[...] = jnp.broadcast_to(s0, m_ref.shape)
        l_ref[...] = jnp.ones_like(l_ref)
        acc_ref[...] = vn_ref[0]
        carry_ref[0:1, :] = lfn_ref[0]

    nrow = k_refs[0].shape[1]
    qb = q.astype(BF16)
    lane = lax.broadcasted_iota(jnp.int32, (D_HEADS, nrow), 1)
    sub = lax.broadcasted_iota(jnp.int32, (D_HEADS, nrow), 0)
    own = (lane & (D_HEADS - 1)) == sub
    carry = carry_ref[0:1, :]
    scores = []
    for j in range(pages):
        s = _dot_nt(qb, k_refs[j][0].astype(BF16)) * scale
        scores.append(jnp.where(own, s + (ex_refs[j][0] + carry), NEG))
        carry = carry + tot_refs[j][0]
    carry_ref[0:1, :] = carry
    mx = jnp.max(scores[0], axis=-1, keepdims=True)
    for s in scores[1:]:
        mx = jnp.maximum(mx, jnp.max(s, axis=-1, keepdims=True))
    m_old = m_ref[:, 0:1]
    m_new = jnp.maximum(m_old, mx)
    alpha = jnp.exp(m_old - m_new)
    lsum = None
    pv = None
    for j in range(pages):
        pm = jnp.exp(scores[j] - m_new)
        ls = jnp.sum(pm, axis=-1, keepdims=True)
        r = _dot(pm.astype(BF16), v_refs[j][0].astype(BF16))
        lsum = ls if lsum is None else lsum + ls
        pv = r if pv is None else pv + r
    l_ref[...] = alpha * l_ref[...] + lsum
    acc_ref[...] = alpha * acc_ref[...] + pv
    m_ref[...] = jnp.broadcast_to(m_new, m_ref.shape)

    @pl.when(p == npg - 1)
    def _():
        o_ref[0] = acc_ref[...] / l_ref[...]


def _paged_attn(page_table, q, kn, vn, lfn_rep, kc, vc, ex, tot, page_base, pages=4):
    m, npg = page_table.shape
    nrow = kc.shape[1]
    pages = math.gcd(npg, pages)

    def pg(j):
        return lambda b, p, pt: (page_base + pt[b, npg - 1 - (p * pages + j)], 0, 0)

    tok = pl.BlockSpec((1, D_HEADS, D_HDIM), lambda b, p, pt: (b, 0, 0))
    flat = pl.BlockSpec((1, 1, nrow), lambda b, p, pt: (b, 0, 0))
    big = [pl.BlockSpec((1, nrow, D_HDIM), pg(j)) for j in range(pages)]
    small = [pl.BlockSpec((1, 1, nrow), pg(j)) for j in range(pages)]
    grid_spec = pltpu.PrefetchScalarGridSpec(
        num_scalar_prefetch=1,
        grid=(m, npg // pages),
        in_specs=[tok, tok, tok, flat] + big + big + small + small,
        out_specs=tok,
        scratch_shapes=[
            pltpu.VMEM((D_HEADS, D_HDIM), F32), pltpu.VMEM((D_HEADS, D_HDIM), F32),
            pltpu.VMEM((D_HEADS, D_HDIM), F32), pltpu.VMEM((SUBLANES, nrow), F32),
        ],
    )
    return pl.pallas_call(
        functools.partial(_paged_attn_kernel, pages=pages),
        grid_spec=grid_spec,
        out_shape=jax.ShapeDtypeStruct((m, D_HEADS, D_HDIM), F32),
        compiler_params=_cparams("parallel", "arbitrary"),
        name="paged_attn",
    )(page_table, q, kn, vn, lfn_rep, *([kc] * pages), *([vc] * pages), *([ex] * pages), *([tot] * pages))


def _pad_lanes(v, n=LANES):
    v = v.reshape(1, -1).astype(F32)
    return jnp.pad(v, ((0, 0), (0, n - v.shape[1])))


def _pad_cols(w, n=LANES):
    return jnp.pad(w, ((0, 0), (0, n - w.shape[1])))


def _row(v):
    return v.reshape(1, -1).astype(F32)


def _expand_matrix():
    e = np.zeros((LANES, D_MODEL), np.float32)
    for h in range(B_HEADS):
        e[h, h * B_HDIM:(h + 1) * B_HDIM] = 1.0
    return jnp.asarray(e, BF16)


def kernel(x_prompt, x_sample, state_ssm, state_ssm_conv, state_lru, state_lru_conv, cache_k, cache_v, cache_logf, state_ffn_conv, page_table, norm_mix, norm_ffn, norm_final, w_in_even, sgu_ln_g, sgu_ln_b, sgu_w, sgu_b, ssd_conv_w, ssd_conv_b, ssd_dt_bias, ssd_a_log, ssd_d, ssd_norm_g, w_out_even, w_in_odd, lru_conv_w, lru_conv_b, lru_w_a, lru_b_a, lru_w_x, lru_b_x, lru_lambda, fox_b_f, w_out_odd, ffn_w_g, ffn_w_u, ffn_conv_w, ffn_conv_b, ffn_w_d):
    nb, seq, d = x_prompt.shape
    ms = x_sample.shape[0]
    depth = norm_mix.shape[0]
    n_pool = cache_k.shape[1]
    xp = x_prompt.reshape(nb * seq, d)
    xs = x_sample.reshape(ms, d)
    gs = B_GROUPS * B_STATE
    main_even = 3 * d + B_CONV_DIM

    res = {k: [] for k in ("sgu_v", "ssm_p", "ssm_s", "sconv_p", "sconv_s", "lru_p", "lru_s", "lconv_p",
                           "lconv_s", "k_p", "v_p", "f_p", "k_s", "v_s", "f_s", "fconv_p", "fconv_s")}

    wg_all = ffn_w_g.astype(BF16)
    wu_all = ffn_w_u.astype(BF16)
    wd_all = ffn_w_d.astype(BF16)

    for l in range(depth):
        li = l // 2
        if l % 2 == 0:
            w_in = w_in_even[li]
            w_bf = w_in.astype(BF16)
            w_dt = _pad_cols(w_in[:, main_even:]).astype(BF16)
            p = {
                "ln_g": _row(sgu_ln_g[li]), "ln_b": _row(sgu_ln_b[li]),
                "sgu_w": sgu_w[li].astype(F32), "sgu_bt": sgu_b[li].astype(F32).T,
                "sgu_w0": _row(jnp.repeat(sgu_w[li][:, 0, 0], A_GDIM)),
                "sgu_b0": _row(jnp.repeat(sgu_b[li][:, 0], A_GDIM)),
                "conv_w": ssd_conv_w[li].astype(F32), "conv_b": _row(ssd_conv_b[li]),
                "dt_bias": _pad_lanes(ssd_dt_bias[li]), "a_log": _pad_lanes(ssd_a_log[li]),
                "a_log_x": _row(jnp.repeat(ssd_a_log[li], B_HDIM)),
                "d_skip": _row(jnp.repeat(ssd_d[li], B_HDIM)), "norm_g": _row(ssd_norm_g[li]),
                "expand": _expand_matrix(),
            }
            w_out = w_out_even[li].astype(BF16)

            proj, dtraw = _rms_matmul(xp, norm_mix[l], w_bf, (BF16,), 0, main_even, w_side=w_dt, tn=1536)
            ya, yb, tails, state = _even_prompt(proj, dtraw, nb, seq, p)
            xp = _matmul_res(ya, yb, w_out, xp)
            res["ssm_p"].append(state.reshape(nb, B_HEADS, B_HDIM, B_STATE))
            res["sconv_p"].append(tails[:, SUBLANES - 3:, :])

            proj_s, dtraw_s = _rms_matmul(xs, norm_mix[l], w_bf, (F32,), 0, main_even, w_side=w_dt)
            prefix = jnp.swapaxes(state_ssm_conv[li], 0, 1)
            ya_s, v_s, npre, xs_s, bm_s, cm_s, dtx_s, dec_s = _even_sample_a(proj_s, dtraw_s, prefix, p)
            hnew, yrow = _ssm_state_sample(
                state_ssm[li].reshape(ms, d, B_STATE), dtx_s.reshape(ms, 1, d), dec_s.reshape(ms, 1, d),
                bm_s.reshape(ms, B_GROUPS, B_STATE), cm_s.reshape(ms, B_GROUPS, B_STATE))
            yb_s = _even_sample_c(yrow.reshape(ms, d), xs_s, proj_s[:, 2 * d:3 * d], p)
            xs = _matmul_res(ya_s, yb_s, w_out, xs)
            res["sgu_v"].append(v_s.reshape(ms, 1, d))
            res["ssm_s"].append(hnew.reshape(ms, B_HEADS, B_HDIM, B_STATE))
            res["sconv_s"].append(jnp.swapaxes(npre, 0, 1))
        else:
            w_in = w_in_odd[li]
            w_bf = w_in.astype(BF16)
            w_f = _pad_cols(w_in[:, 5 * d:]).astype(BF16)
            p = {
                "conv_w": lru_conv_w[li].astype(F32), "conv_b": _row(lru_conv_b[li]),
                "w_a": lru_w_a[li].astype(BF16), "b_a": _row(lru_b_a[li]),
                "w_x": lru_w_x[li].astype(BF16), "b_x": _row(lru_b_x[li]),
                "lam": _row(lru_lambda[li]), "b_f": _pad_lanes(fox_b_f[li]),
            }
            w_out = w_out_odd[li].astype(BF16)

            proj, fraw = _rms_matmul(xp, norm_mix[l], w_bf, (BF16,), 0, 3 * d, w_side=w_f, tn=2048)
            kf, kb = _kv_proj(xp, norm_mix[l], w_bf, 3 * d)
            vf, vb = _kv_proj(xp, norm_mix[l], w_bf, 4 * d)
            yc, lf, ccol, crow, last, ctail = _lru_prompt(proj, fraw, nb, seq, p)
            yd = _attn_prompt(proj, kb, vb, ccol, crow, nb, seq)
            xp = _matmul_res(yc, yd, w_out, xp)
            res["lru_p"].append(last.reshape(nb, d))
            res["lconv_p"].append(ctail[:, SUBLANES - 3:, :])
            res["k_p"].append(kf.reshape(nb, seq, D_HEADS, D_HDIM))
            res["v_p"].append(vf.reshape(nb, seq, D_HEADS, D_HDIM))
            res["f_p"].append(lf[:, :D_HEADS].reshape(nb, seq, D_HEADS))

            proj_s, fraw_s = _rms_matmul(xs, norm_mix[l], w_bf, (F32,), 0, 5 * d, w_side=w_f)
            prefix = jnp.swapaxes(state_lru_conv[li], 0, 1)
            yc_s, hn_s, npre, lf_s = _odd_sample(proj_s[:, :d], proj_s[:, d:2 * d], fraw_s, prefix,
                                                 state_lru[li].astype(F32), p)
            q_s = proj_s[:, 2 * d:3 * d].reshape(ms, D_HEADS, D_HDIM)
            k_s = proj_s[:, 3 * d:4 * d].reshape(ms, D_HEADS, D_HDIM)
            v_s = proj_s[:, 4 * d:5 * d].reshape(ms, D_HEADS, D_HDIM)
            lfn = lf_s[:, :D_HEADS]
            nrow = PAGE * D_HEADS
            ex, tot = _page_bias(cache_logf[li].reshape(n_pool, nrow).astype(F32))
            yd_s = _paged_attn(
                page_table, q_s, k_s, v_s, jnp.tile(lfn, (1, PAGE)).reshape(ms, 1, nrow),
                cache_k.reshape(-1, nrow, D_HDIM), cache_v.reshape(-1, nrow, D_HDIM),
                ex.reshape(n_pool, 1, nrow), tot.reshape(n_pool, 1, nrow), li * n_pool)
            xs = _matmul_res(yc_s, yd_s.reshape(ms, d).astype(BF16), w_out, xs)
            res["lru_s"].append(hn_s)
            res["lconv_s"].append(jnp.swapaxes(npre, 0, 1))
            res["k_s"].append(k_s.reshape(ms, 1, D_HEADS, D_HDIM))
            res["v_s"].append(v_s.reshape(ms, 1, D_HEADS, D_HDIM))
            res["f_s"].append(lfn.reshape(ms, 1, D_HEADS))

        last_layer = l == depth - 1
        cw = ffn_conv_w[l].astype(F32)
        xp, ftails = _ffn_prompt(xp, norm_ffn[l], wg_all, wu_all, cw, ffn_conv_b[l], wd_all, norm_final, l, seq,
                                 last_layer)
        xs, fpre = _ffn_sample(xs, norm_ffn[l], wg_all, wu_all, cw, ffn_conv_b[l],
                               jnp.swapaxes(state_ffn_conv[l], 0, 1).astype(F32), wd_all, norm_final, l,
                               last_layer)
        res["fconv_p"].append(ftails[:, SUBLANES - 2:, :])
        res["fconv_s"].append(jnp.swapaxes(fpre, 0, 1))

    st = lambda k: jnp.stack(res[k])
    return (xp.reshape(nb, seq, d), xs.reshape(ms, 1, d), st("sgu_v"), st("ssm_p"), st("ssm_s"), st("sconv_p"),
            st("sconv_s"), st("lru_p"), st("lru_s"), st("lconv_p"), st("lconv_s"), st("k_p"), st("v_p"),
            st("f_p"), st("k_s"), st("v_s"), st("f_s"), st("fconv_p"), st("fconv_s"))
```

```python
import functools
import math

import jax
import jax.numpy as jnp
import numpy as np
from jax import lax
from jax.experimental import pallas as pl
from jax.experimental.pallas import tpu as pltpu

F32 = jnp.float32
BF16 = jnp.bfloat16

D_MODEL = 2048
PAGE = 128
A_GROUPS = 8
A_GDIM = D_MODEL // A_GROUPS
CHUNK = 128
B_HEADS = 32
B_HDIM = 64
B_GROUPS = 4
B_GW = D_MODEL // B_GROUPS
B_STATE = 128
B_CONV_DIM = D_MODEL + 2 * B_GROUPS * B_STATE
C_HEADS = 8
C_BLOCK = D_MODEL // C_HEADS
C_GATE = 8.0
D_HEADS = 16
D_HDIM = 128
D_FF = 5632
EPS = 1e-6
NEG = -1e30
LOG2E = 1.4426950408889634
GELU_C1 = 2.0 * math.sqrt(2.0 / math.pi)
GELU_C3 = GELU_C1 * 0.044715

LANES = 128
SUBLANES = 8
VMEM_LIMIT = 56 * 1024 * 1024


def _cparams(*sem):
    return pltpu.CompilerParams(dimension_semantics=sem, vmem_limit_bytes=VMEM_LIMIT)


def _gelu(x):
    y2 = x * (GELU_C1 + GELU_C3 * (x * x))
    return x / (1.0 + jnp.exp(-y2))


def _silu(x):
    return x * jax.nn.sigmoid(x)


def _softplus(x):
    return jnp.maximum(x, 0.0) + jnp.log1p(jnp.exp(-jnp.abs(x)))


def _log_sigmoid(x):
    return -_softplus(-x)


def _dot(a, b):
    return jnp.dot(a, b, preferred_element_type=F32)


def _dot_nt(a, b):
    return lax.dot_general(a, b, (((1,), (1,)), ((), ())), preferred_element_type=F32)


def _shift_rows(x, k, carry):
    rolled = pltpu.roll(x, k, 0)
    croll = pltpu.roll(carry, k, 0)
    rows = lax.broadcasted_iota(jnp.int32, carry.shape, 0)
    top = jnp.where(rows < k, croll, rolled[:SUBLANES])
    return jnp.concatenate([top, rolled[SUBLANES:]], axis=0)


def _causal_conv(x, carry, w_ref, b_ref):
    width = w_ref.shape[0]
    y = b_ref[...] + x * w_ref[width - 1:width, :]
    for k in range(1, width):
        y = y + _shift_rows(x, k, carry) * w_ref[width - 1 - k:width - k, :]
    return y


def _cumsum_rows(x):
    n = x.shape[0]
    rows = lax.broadcasted_iota(jnp.int32, x.shape, 0)
    s = 1
    while s < n:
        x = x + jnp.where(rows >= s, pltpu.roll(x, s, 0), 0.0)
        s *= 2
    return x


def _expand_heads(x, e_ref):
    e = e_ref[...]
    x1 = x.astype(BF16)
    r1 = x - x1.astype(F32)
    x2 = r1.astype(BF16)
    x3 = (r1 - x2.astype(F32)).astype(BF16)
    return _dot(x1, e) + _dot(x2, e) + _dot(x3, e)


def _rmsnorm(x, g):
    ms = jnp.mean(x * x, axis=-1, keepdims=True)
    return x * lax.rsqrt(ms + EPS) * g


def _rms_matmul_kernel(*refs, n_out, has_side):
    x_ref, g_ref, w_ref = refs[:3]
    pos = 3
    ws_ref = None
    if has_side:
        ws_ref = refs[pos]
        pos += 1
    out_refs = refs[pos:pos + n_out]
    pos += n_out
    side_ref = None
    if has_side:
        side_ref = refs[pos]
        pos += 1
    hn_ref = refs[pos]

    @pl.when(pl.program_id(1) == 0)
    def _():
        hn = _rmsnorm(x_ref[...], g_ref[...]).astype(BF16)
        hn_ref[...] = hn
        if has_side:
            side_ref[...] = _dot(hn, ws_ref[...])

    acc = _dot(hn_ref[...], w_ref[...])
    for o in out_refs:
        o[...] = acc.astype(o.dtype)


def _rms_matmul(x, g, w, out_dtypes, col0, n, w_side=None, tm=1024, tn=1024):
    m, d = x.shape
    tm = min(tm, m)
    tn = min(tn, n)
    assert m % tm == 0 and n % tn == 0 and col0 % tn == 0
    jb = col0 // tn
    has_side = w_side is not None
    in_specs = [
        pl.BlockSpec((tm, d), lambda i, j: (i, 0)),
        pl.BlockSpec((1, d), lambda i, j: (0, 0)),
        pl.BlockSpec((d, tn), lambda i, j: (0, jb + j)),
    ]
    args = [x, g.reshape(1, d), w]
    out_shape = [jax.ShapeDtypeStruct((m, n), dt) for dt in out_dtypes]
    out_specs = [pl.BlockSpec((tm, tn), lambda i, j: (i, j)) for _ in out_dtypes]
    if has_side:
        ns = w_side.shape[1]
        in_specs.append(pl.BlockSpec((d, ns), lambda i, j: (0, 0)))
        args.append(w_side)
        out_shape.append(jax.ShapeDtypeStruct((m, ns), F32))
        out_specs.append(pl.BlockSpec((tm, ns), lambda i, j: (i, 0)))
    outs = pl.pallas_call(
        functools.partial(_rms_matmul_kernel, n_out=len(out_dtypes), has_side=has_side),
        grid=(m // tm, n // tn),
        in_specs=in_specs,
        out_specs=out_specs,
        out_shape=out_shape,
        scratch_shapes=[pltpu.VMEM((tm, d), BF16)],
        compiler_params=_cparams("parallel", "arbitrary"),
        name="rms_matmul",
    )(*args)
    return outs


def _kv_proj_kernel(x_ref, g_ref, wk_ref, wv_ref, kf_ref, kb_ref, vf_ref, vb_ref):
    hn = _rmsnorm(x_ref[...], g_ref[...]).astype(BF16)
    tm = hn.shape[0]
    for w_ref, of_ref, ob_ref in ((wk_ref, kf_ref, kb_ref), (wv_ref, vf_ref, vb_ref)):
        acc = _dot(hn, w_ref[...])
        ob_ref[...] = acc.astype(BF16)
        for h in range(D_HEADS):
            of_ref[pl.ds(h, tm, stride=D_HEADS), :] = acc[:, h * D_HDIM:(h + 1) * D_HDIM]


def _kv_proj(x, g, w, col_k, col_v, tm=512):
    m, d = x.shape
    n = D_HEADS * D_HDIM
    tm = min(tm, m)
    assert m % tm == 0 and col_k % n == 0 and col_v % n == 0
    jk, jv = col_k // n, col_v // n
    head_major = pl.BlockSpec((tm * D_HEADS, D_HDIM), lambda i: (i, 0))
    token_major = pl.BlockSpec((tm, n), lambda i: (i, 0))
    return pl.pallas_call(
        _kv_proj_kernel,
        grid=(m // tm,),
        in_specs=[
            pl.BlockSpec((tm, d), lambda i: (i, 0)),
            pl.BlockSpec((1, d), lambda i: (0, 0)),
            pl.BlockSpec((d, n), lambda i: (0, jk), pipeline_mode=pl.Buffered(1)),
            pl.BlockSpec((d, n), lambda i: (0, jv), pipeline_mode=pl.Buffered(1)),
        ],
        out_specs=[head_major, token_major, head_major, token_major],
        out_shape=[
            jax.ShapeDtypeStruct((m * D_HEADS, D_HDIM), F32),
            jax.ShapeDtypeStruct((m, n), BF16),
            jax.ShapeDtypeStruct((m * D_HEADS, D_HDIM), F32),
            jax.ShapeDtypeStruct((m, n), BF16),
        ],
        compiler_params=_cparams("parallel"),
        name="kv_proj",
    )(x, g.reshape(1, d), w, w)


def _matmul_res_kernel(a1_ref, a2_ref, w1_ref, w2_ref, r_ref, o_ref):
    o_ref[...] = r_ref[...] + _dot(a1_ref[...], w1_ref[...]) + _dot(a2_ref[...], w2_ref[...])


def _matmul_res(a1, a2, w, res, tm=512):
    m, k1 = a1.shape
    k2 = a2.shape[1]
    n = w.shape[1]
    tm = min(tm, m)
    assert m % tm == 0 and k1 == k2
    once = pl.Buffered(1)
    return pl.pallas_call(
        _matmul_res_kernel,
        grid=(m // tm,),
        in_specs=[
            pl.BlockSpec((tm, k1), lambda i: (i, 0)),
            pl.BlockSpec((tm, k2), lambda i: (i, 0)),
            pl.BlockSpec((k1, n), lambda i: (0, 0), pipeline_mode=once),
            pl.BlockSpec((k2, n), lambda i: (1, 0), pipeline_mode=once),
            pl.BlockSpec((tm, n), lambda i: (i, 0)),
        ],
        out_specs=pl.BlockSpec((tm, n), lambda i: (i, 0)),
        out_shape=jax.ShapeDtypeStruct((m, n), F32),
        compiler_params=_cparams("parallel"),
        name="matmul_res",
    )(a1, a2, w, w, res)


def _ffn_prompt_kernel(x_ref, g_ref, wg_ref, wu_ref, cw_ref, cb_ref, wd_ref, gf_ref,
                       o_ref, tail_ref, hn_ref, carry_ref, *, tiles_per_seq, final_norm):
    i = pl.program_id(0)
    j = pl.program_id(1)
    nj = pl.num_programs(1)

    @pl.when(j == 0)
    def _():
        x = x_ref[...]
        hn_ref[...] = _rmsnorm(x, g_ref[...]).astype(BF16)
        o_ref[...] = x

    @pl.when(i % tiles_per_seq == 0)
    def _():
        carry_ref[j] = jnp.zeros(carry_ref.shape[1:], F32)

    hn = hn_ref[...]
    graw = _dot(hn, wg_ref[...])
    up = _dot(hn, wu_ref[...])
    gc = _causal_conv(graw, carry_ref[j], cw_ref, cb_ref)
    tail = graw[graw.shape[0] - SUBLANES:, :]
    carry_ref[j] = tail
    tail_ref[0] = tail
    act = (_silu(gc) * up).astype(BF16)
    o_ref[...] += _dot(act, wd_ref[...])

    if final_norm:
        @pl.when(j == nj - 1)
        def _():
            o_ref[...] = _rmsnorm(o_ref[...], gf_ref[...])


def _ffn_prompt(x, g, wg, wu, cw, cb, wd, gf, layer, seq, final_norm, tm=1024, tf=512):
    m, d = x.shape
    ff = wg.shape[2]
    tm = min(tm, seq)
    assert m % tm == 0 and ff % tf == 0 and seq % tm == 0
    tiles_per_seq = seq // tm
    out, tails = pl.pallas_call(
        functools.partial(_ffn_prompt_kernel, tiles_per_seq=tiles_per_seq, final_norm=final_norm),
        grid=(m // tm, ff // tf),
        in_specs=[
            pl.BlockSpec((tm, d), lambda i, j: (i, 0)),
            pl.BlockSpec((1, d), lambda i, j: (0, 0)),
            pl.BlockSpec((None, d, tf), lambda i, j: (layer, 0, j)),
            pl.BlockSpec((None, d, tf), lambda i, j: (layer, 0, j)),
            pl.BlockSpec((cw.shape[0], tf), lambda i, j: (0, j)),
            pl.BlockSpec((1, tf), lambda i, j: (0, j)),
            pl.BlockSpec((None, tf, d), lambda i, j: (layer, j, 0)),
            pl.BlockSpec((1, d), lambda i, j: (0, 0)),
        ],
        out_specs=[
            pl.BlockSpec((tm, d), lambda i, j: (i, 0)),
            pl.BlockSpec((1, SUBLANES, tf), lambda i, j: (i, 0, j)),
        ],
        out_shape=[
            jax.ShapeDtypeStruct((m, d), F32),
            jax.ShapeDtypeStruct((m // tm, SUBLANES, ff), F32),
        ],
        scratch_shapes=[
            pltpu.VMEM((tm, d), BF16),
            pltpu.VMEM((ff // tf, SUBLANES, tf), F32),
        ],
        compiler_params=_cparams("arbitrary", "arbitrary"),
        name="ffn_prompt",
    )(x, g.reshape(1, d), wg, wu, cw, cb.reshape(1, ff), wd, gf.reshape(1, d))
    return out, tails[tiles_per_seq - 1::tiles_per_seq]


def _ffn_sample_kernel(x_ref, g_ref, wg_ref, wu_ref, cw_ref, cb_ref, pre_ref, wd_ref, gf_ref,
                       o_ref, npre_ref, hn_ref, acc_ref, *, final_norm):
    j = pl.program_id(0)
    nj = pl.num_programs(0)

    @pl.when(j == 0)
    def _():
        hn_ref[...] = _rmsnorm(x_ref[...], g_ref[...]).astype(BF16)
        acc_ref[...] = jnp.zeros_like(acc_ref)

    hn = hn_ref[...]
    graw = _dot(hn, wg_ref[...])
    up = _dot(hn, wu_ref[...])
    p0 = pre_ref[0]
    p1 = pre_ref[1]
    gc = cb_ref[...] + graw * cw_ref[2:3, :] + p1 * cw_ref[1:2, :] + p0 * cw_ref[0:1, :]
    npre_ref[0] = p1
    npre_ref[1] = graw
    act = (_silu(gc) * up).astype(BF16)
    acc_ref[...] += _dot(act, wd_ref[...])

    @pl.when(j == nj - 1)
    def _():
        y = x_ref[...] + acc_ref[...]
        if final_norm:
            y = _rmsnorm(y, gf_ref[...])
        o_ref[...] = y


def _ffn_sample(x, g, wg, wu, cw, cb, prefix, wd, gf, layer, final_norm, tf=512):
    m, d = x.shape
    ff = wg.shape[2]
    npre = prefix.shape[0]
    out, new_prefix = pl.pallas_call(
        functools.partial(_ffn_sample_kernel, final_norm=final_norm),
        grid=(ff // tf,),
        in_specs=[
            pl.BlockSpec((m, d), lambda j: (0, 0)),
            pl.BlockSpec((1, d), lambda j: (0, 0)),
            pl.BlockSpec((None, d, tf), lambda j: (layer, 0, j)),
            pl.BlockSpec((None, d, tf), lambda j: (layer, 0, j)),
            pl.BlockSpec((cw.shape[0], tf), lambda j: (0, j)),
            pl.BlockSpec((1, tf), lambda j: (0, j)),
            pl.BlockSpec((npre, m, tf), lambda j: (0, 0, j)),
            pl.BlockSpec((None, tf, d), lambda j: (layer, j, 0)),
            pl.BlockSpec((1, d), lambda j: (0, 0)),
        ],
        out_specs=[
            pl.BlockSpec((m, d), lambda j: (0, 0)),
            pl.BlockSpec((npre, m, tf), lambda j: (0, 0, j)),
        ],
        out_shape=[
            jax.ShapeDtypeStruct((m, d), F32),
            jax.ShapeDtypeStruct((npre, m, ff), F32),
        ],
        scratch_shapes=[pltpu.VMEM((m, d), BF16), pltpu.VMEM((m, d), F32)],
        compiler_params=_cparams("arbitrary"),
        name="ffn_sample",
    )(x, g.reshape(1, d), wg, wu, cw, cb.reshape(1, ff), prefix, wd, gf.reshape(1, d))
    return out, new_prefix


def _even_prompt_kernel(au_ref, av_ref, z_ref, xbc_ref, dtr_ref,
                        lng_ref, lnb_ref, sw_ref, sbt_ref, cw_ref, cb_ref, dtb_ref, alog_ref,
                        dsk_ref, ng_ref, e_ref,
                        ya_ref, yb_ref, tail_ref, st_ref,
                        ht_ref, carry_ref):
    c = pl.program_id(1)
    nc = pl.num_programs(1)

    @pl.when(c == 0)
    def _():
        ht_ref[...] = jnp.zeros_like(ht_ref)
        carry_ref[...] = jnp.zeros_like(carry_ref)

    ht = ht_ref[...]
    carry = carry_ref[...]
    for s in range(au_ref.shape[0] // CHUNK):
        rs = slice(s * CHUNK, (s + 1) * CHUNK)
        ht, carry = _even_chunk(rs, ht, carry, au_ref, av_ref, z_ref, xbc_ref, dtr_ref, lng_ref, lnb_ref, sw_ref,
                                sbt_ref, cw_ref, cb_ref, dtb_ref, alog_ref, dsk_ref, ng_ref, e_ref, ya_ref, yb_ref)
    ht_ref[...] = ht
    carry_ref[...] = carry

    @pl.when(c == nc - 1)
    def _():
        tail_ref[0] = carry
        st_ref[0] = ht.T


def _even_chunk(rs, ht_old, carry, au_ref, av_ref, z_ref, xbc_ref, dtr_ref, lng_ref, lnb_ref, sw_ref, sbt_ref,
                cw_ref, cb_ref, dtb_ref, alog_ref, dsk_ref, ng_ref, e_ref, ya_ref, yb_ref):
    rows = lax.broadcasted_iota(jnp.int32, (CHUNK, CHUNK), 0)
    cols = lax.broadcasted_iota(jnp.int32, (CHUNK, CHUNK), 1)
    tri = rows >= cols

    u = _gelu(au_ref[rs, :].astype(F32))
    gv = _gelu(av_ref[rs, :].astype(F32))
    mu = jnp.mean(gv, axis=-1, keepdims=True)
    xc = gv - mu
    var = jnp.mean(xc * xc, axis=-1, keepdims=True)
    v = xc * lax.rsqrt(var + EPS) * lng_ref[...] + lnb_ref[...]
    vb = v.astype(BF16)
    parts = []
    for g in range(A_GROUPS):
        wg = jnp.where(tri, sw_ref[g], 0.0).astype(BF16)
        parts.append(_dot(wg, vb[:, g * A_GDIM:(g + 1) * A_GDIM]) + sbt_ref[:, g:g + 1])
    ya_ref[rs, :] = (u * jnp.concatenate(parts, axis=1)).astype(BF16)

    xr = xbc_ref[rs, :].astype(F32)
    xbc = _silu(_causal_conv(xr, carry, cw_ref, cb_ref))
    xs = xbc[:, :D_MODEL]
    bm = xbc[:, D_MODEL:D_MODEL + B_GROUPS * B_STATE]
    cm = xbc[:, D_MODEL + B_GROUPS * B_STATE:]

    dt = _softplus(dtr_ref[rs, :] + dtb_ref[...])
    a = -jnp.exp(alog_ref[...])
    acs = _cumsum_rows(dt * a)
    ea = jnp.exp(acs)
    te = jnp.exp(acs[CHUNK - 1:CHUNK, :] - acs) * dt
    ea_x = _expand_heads(ea, e_ref)
    te_x = _expand_heads(te, e_ref)
    acs_t = acs.T
    dt_t = dt.T

    lane = lax.broadcasted_iota(jnp.int32, xs.shape, 1)
    lo = (lane & (2 * B_HDIM - 1)) < B_HDIM
    xs_half = (jnp.where(lo, xs, 0.0).astype(BF16), jnp.where(lo, 0.0, xs).astype(BF16))
    xsc_b = (xs * te_x).astype(BF16)
    ht_b = ht_old.astype(BF16)

    ys = []
    states = []
    for g in range(B_GROUPS):
        gsl = slice(g * B_GW, (g + 1) * B_GW)
        bg = bm[:, g * B_STATE:(g + 1) * B_STATE]
        cg_b = cm[:, g * B_STATE:(g + 1) * B_STATE].astype(BF16)
        cb = _dot_nt(cg_b, bg.astype(BF16))
        y_off = _dot(cg_b, ht_b[:, gsl]) * ea_x[:, gsl]
        states.append(_dot(bg.T.astype(BF16), xsc_b[:, gsl]))
        pair_out = []
        for jp in range(B_GW // (2 * B_HDIM)):
            h0 = g * (B_GW // B_HDIM) + 2 * jp
            acc = None
            for e in range(2):
                h = h0 + e
                seg = acs[:, h:h + 1] - acs_t[h:h + 1, :]
                dec = jnp.exp(jnp.where(tri, seg, -jnp.inf))
                mh = (cb * dec * dt_t[h:h + 1, :]).astype(BF16)
                r = _dot(mh, xs_half[e][:, h0 * B_HDIM:(h0 + 2) * B_HDIM])
                acc = r if acc is None else acc + r
            pair_out.append(acc)
        ys.append(jnp.concatenate(pair_out, axis=1) + y_off)
    y = jnp.concatenate(ys, axis=1)
    ht_new = ea_x[CHUNK - 1:CHUNK, :] * ht_old + jnp.concatenate(states, axis=1)

    y = (y + dsk_ref[...] * xs) * _silu(z_ref[rs, :].astype(F32))
    outs = []
    for g in range(B_GROUPS):
        yg = y[:, g * B_GW:(g + 1) * B_GW]
        ms = jnp.mean(yg * yg, axis=-1, keepdims=True)
        outs.append(yg * lax.rsqrt(ms + EPS))
    yb_ref[rs, :] = (jnp.concatenate(outs, axis=1) * ng_ref[...]).astype(BF16)
    return ht_new, xr[CHUNK - SUBLANES:, :]


def _even_prompt(proj, dtraw, nb, seq, p, chunks_per_step=2):
    n = proj.shape[0]
    tr = CHUNK * chunks_per_step
    assert seq % tr == 0
    nc = seq // tr
    d = D_MODEL

    def row(b, c):
        return b * nc + c

    full = lambda shape: pl.BlockSpec(shape, lambda b, c: (0,) * len(shape))
    ya, yb, tails, state = pl.pallas_call(
        _even_prompt_kernel,
        grid=(nb, nc),
        in_specs=[
            pl.BlockSpec((tr, d), lambda b, c: (row(b, c), 0)),
            pl.BlockSpec((tr, d), lambda b, c: (row(b, c), 1)),
            pl.BlockSpec((tr, d), lambda b, c: (row(b, c), 2)),
            pl.BlockSpec((tr, B_CONV_DIM), lambda b, c: (row(b, c), 2)),
            pl.BlockSpec((tr, LANES), lambda b, c: (row(b, c), 0)),
            full((1, d)), full((1, d)), full((A_GROUPS, CHUNK, CHUNK)), full((CHUNK, A_GROUPS)),
            full((4, B_CONV_DIM)), full((1, B_CONV_DIM)), full((1, LANES)), full((1, LANES)),
            full((1, d)), full((1, d)), full((LANES, d)),
        ],
        out_specs=[
            pl.BlockSpec((tr, d), lambda b, c: (row(b, c), 0)),
            pl.BlockSpec((tr, d), lambda b, c: (row(b, c), 0)),
            pl.BlockSpec((1, SUBLANES, B_CONV_DIM), lambda b, c: (b, 0, 0)),
            pl.BlockSpec((1, d, B_STATE), lambda b, c: (b, 0, 0)),
        ],
        out_shape=[
            jax.ShapeDtypeStruct((n, d), BF16),
            jax.ShapeDtypeStruct((n, d), BF16),
            jax.ShapeDtypeStruct((nb, SUBLANES, B_CONV_DIM), F32),
            jax.ShapeDtypeStruct((nb, d, B_STATE), F32),
        ],
        scratch_shapes=[pltpu.VMEM((B_STATE, d), F32), pltpu.VMEM((SUBLANES, B_CONV_DIM), F32)],
        compiler_params=_cparams("parallel", "arbitrary"),
        name="even_prompt",
    )(proj, proj, proj, proj, dtraw,
      p["ln_g"], p["ln_b"], p["sgu_w"], p["sgu_bt"], p["conv_w"], p["conv_b"], p["dt_bias"], p["a_log"],
      p["d_skip"], p["norm_g"], p["expand"])
    return ya, yb, tails, state


def _even_sample_a_kernel(proj_ref, dtr_ref, pre_ref, lng_ref, lnb_ref, w0_ref, b0_ref, cw_ref, cb_ref,
                          dtb_ref, alogx_ref, e_ref,
                          ya_ref, v_ref, npre_ref, xs_ref, bm_ref, cm_ref, dtx_ref, dec_ref):
    d = D_MODEL
    u = _gelu(proj_ref[:, 0:d])
    gv = _gelu(proj_ref[:, d:2 * d])
    mu = jnp.mean(gv, axis=-1, keepdims=True)
    xc = gv - mu
    var = jnp.mean(xc * xc, axis=-1, keepdims=True)
    v = xc * lax.rsqrt(var + EPS) * lng_ref[...] + lnb_ref[...]
    v_ref[...] = v
    ya_ref[...] = (u * (w0_ref[...] * v + b0_ref[...])).astype(BF16)

    xr = proj_ref[:, 3 * d:3 * d + B_CONV_DIM]
    p0, p1, p2 = pre_ref[0], pre_ref[1], pre_ref[2]
    y = (cb_ref[...] + xr * cw_ref[3:4, :] + p2 * cw_ref[2:3, :] + p1 * cw_ref[1:2, :]
         + p0 * cw_ref[0:1, :])
    npre_ref[0] = p1
    npre_ref[1] = p2
    npre_ref[2] = xr
    xbc = _silu(y)
    xs = xbc[:, :d]
    xs_ref[...] = xs
    bm_ref[...] = xbc[:, d:d + B_GROUPS * B_STATE]
    cm_ref[...] = xbc[:, d + B_GROUPS * B_STATE:]
    dt_x = _expand_heads(_softplus(dtr_ref[...] + dtb_ref[...]), e_ref)
    dtx_ref[...] = dt_x * xs
    dec_ref[...] = jnp.exp(dt_x * (-jnp.exp(alogx_ref[...])))


def _even_sample_a(proj, dtraw, prefix, p):
    m = proj.shape[0]
    d = D_MODEL
    gs = B_GROUPS * B_STATE
    return pl.pallas_call(
        _even_sample_a_kernel,
        out_shape=[
            jax.ShapeDtypeStruct((m, d), BF16),
            jax.ShapeDtypeStruct((m, d), F32),
            jax.ShapeDtypeStruct(prefix.shape, F32),
            jax.ShapeDtypeStruct((m, d), F32),
            jax.ShapeDtypeStruct((m, gs), F32),
            jax.ShapeDtypeStruct((m, gs), F32),
            jax.ShapeDtypeStruct((m, d), F32),
            jax.ShapeDtypeStruct((m, d), F32),
        ],
        compiler_params=pltpu.CompilerParams(vmem_limit_bytes=VMEM_LIMIT),
        name="even_sample_a",
    )(proj, dtraw, prefix, p["ln_g"], p["ln_b"], p["sgu_w0"], p["sgu_b0"], p["conv_w"], p["conv_b"],
      p["dt_bias"], p["a_log_x"], p["expand"])


def _ssm_state_sample_kernel(h_ref, dtx_ref, dec_ref, bm_ref, cm_ref, hn_ref, y_ref):
    ns = h_ref.shape[2]
    for g in range(B_GROUPS):
        sl = slice(g * B_GW, (g + 1) * B_GW)
        dec_col = jnp.broadcast_to(dec_ref[0, :, sl], (ns, B_GW)).T
        dtx_col = jnp.broadcast_to(dtx_ref[0, :, sl], (ns, B_GW)).T
        hn = dec_col * h_ref[0, sl, :] + dtx_col * bm_ref[0, g:g + 1, :]
        hn_ref[0, sl, :] = hn
        c_rows = jnp.broadcast_to(cm_ref[0, g:g + 1, :], (SUBLANES, ns)).astype(BF16)
        y_ref[0, :, sl] = _dot_nt(c_rows, hn.astype(BF16))[0:1, :]


def _ssm_state_sample(h0, dtx, dec, bm, cm):
    m, d, ns = h0.shape
    row = pl.BlockSpec((1, 1, d), lambda b: (b, 0, 0))
    grp = pl.BlockSpec((1, B_GROUPS, ns), lambda b: (b, 0, 0))
    st = pl.BlockSpec((1, d, ns), lambda b: (b, 0, 0))
    return pl.pallas_call(
        _ssm_state_sample_kernel,
        grid=(m,),
        in_specs=[st, row, row, grp, grp],
        out_specs=[st, row],
        out_shape=[jax.ShapeDtypeStruct((m, d, ns), F32), jax.ShapeDtypeStruct((m, 1, d), F32)],
        compiler_params=_cparams("parallel"),
        name="ssm_state_sample",
    )(h0, dtx, dec, bm, cm)


def _even_sample_c_kernel(y_ref, xs_ref, z_ref, dsk_ref, ng_ref, yb_ref):
    y = (y_ref[...] + dsk_ref[...] * xs_ref[...]) * _silu(z_ref[...])
    outs = []
    for g in range(B_GROUPS):
        yg = y[:, g * B_GW:(g + 1) * B_GW]
        ms = jnp.mean(yg * yg, axis=-1, keepdims=True)
        outs.append(yg * lax.rsqrt(ms + EPS))
    yb_ref[...] = (jnp.concatenate(outs, axis=1) * ng_ref[...]).astype(BF16)


def _even_sample_c(y, xs, z, p):
    return pl.pallas_call(
        _even_sample_c_kernel,
        out_shape=jax.ShapeDtypeStruct(y.shape, BF16),
        name="even_sample_c",
    )(y, xs, z, p["d_skip"], p["norm_g"])


def _lru_gates(xc, wa_ref, ba_ref, wx_ref, bx_ref, lam_ref):
    xcb = xc.astype(BF16)
    rp, ip = [], []
    for h in range(C_HEADS):
        xh = xcb[:, h * C_BLOCK:(h + 1) * C_BLOCK]
        rp.append(_dot(xh, wa_ref[h]))
        ip.append(_dot(xh, wx_ref[h]))
    r = jax.nn.sigmoid(jnp.concatenate(rp, axis=1) + ba_ref[...])
    ig = jax.nn.sigmoid(jnp.concatenate(ip, axis=1) + bx_ref[...])
    log_a = -C_GATE * r * _softplus(-lam_ref[...])
    a = jnp.exp(log_a)
    bt = jnp.sqrt(-jnp.tanh(log_a) * (a * a + 1.0)) * (ig * xc)
    return a, bt


def _lru_prompt_kernel(gy_ref, gx_ref, fr_ref, cw_ref, cb_ref, wa_ref, ba_ref, wx_ref, bx_ref, lam_ref, bf_ref,
                       yc_ref, lf_ref, ccol_ref, crow_ref, last_ref, tail_ref,
                       a_s, b_s, h_s, hc_ref, cc_ref, fc_ref):
    t = pl.program_id(1)
    nt = pl.num_programs(1)
    tm = gy_ref.shape[0]

    @pl.when(t == 0)
    def _():
        hc_ref[...] = jnp.zeros_like(hc_ref)
        cc_ref[...] = jnp.zeros_like(cc_ref)
        fc_ref[...] = jnp.zeros_like(fc_ref)

    xr = gx_ref[...].astype(F32)
    xc = _causal_conv(xr, cc_ref[...], cw_ref, cb_ref)
    cc_ref[...] = xr[tm - SUBLANES:, :]
    a, bt = _lru_gates(xc, wa_ref, ba_ref, wx_ref, bx_ref, lam_ref)
    a_s[...] = a
    b_s[...] = bt

    def body(i, h):
        h = a_s[pl.ds(i, 1), :] * h + b_s[pl.ds(i, 1), :]
        h_s[pl.ds(i, 1), :] = h
        return h

    h = lax.fori_loop(0, tm, body, hc_ref[0:1, :], unroll=8)
    hc_ref[0:1, :] = h
    yc_ref[...] = (h_s[...] * _gelu(gy_ref[...].astype(F32))).astype(BF16)

    lf = _log_sigmoid(fr_ref[...] + bf_ref[...])
    cs = _cumsum_rows(lf) + fc_ref[0:1, :]
    fc_ref[0:1, :] = cs[tm - 1:tm, :]
    lf_ref[...] = lf
    ccol_ref[...] = cs
    crow_ref[0] = cs.T[:D_HEADS, :]

    @pl.when(t == nt - 1)
    def _():
        last_ref[0] = h
        tail_ref[0] = xr[tm - SUBLANES:, :]


def _lru_prompt(proj, fraw, nb, seq, p, tm=256):
    n = proj.shape[0]
    d = D_MODEL
    nt = seq // tm

    def row(b, t):
        return b * nt + t

    full = lambda shape: pl.BlockSpec(shape, lambda b, t: (0,) * len(shape))
    return pl.pallas_call(
        _lru_prompt_kernel,
        grid=(nb, nt),
        in_specs=[
            pl.BlockSpec((tm, d), lambda b, t: (row(b, t), 0)),
            pl.BlockSpec((tm, d), lambda b, t: (row(b, t), 1)),
            pl.BlockSpec((tm, LANES), lambda b, t: (row(b, t), 0)),
            full((4, d)), full((1, d)),
            full((C_HEADS, C_BLOCK, C_BLOCK)), full((1, d)),
            full((C_HEADS, C_BLOCK, C_BLOCK)), full((1, d)),
            full((1, d)), full((1, LANES)),
        ],
        out_specs=[
            pl.BlockSpec((tm, d), lambda b, t: (row(b, t), 0)),
            pl.BlockSpec((tm, LANES), lambda b, t: (row(b, t), 0)),
            pl.BlockSpec((tm, LANES), lambda b, t: (row(b, t), 0)),
            pl.BlockSpec((1, D_HEADS, tm), lambda b, t: (b, 0, t)),
            pl.BlockSpec((1, 1, d), lambda b, t: (b, 0, 0)),
            pl.BlockSpec((1, SUBLANES, d), lambda b, t: (b, 0, 0)),
        ],
        out_shape=[
            jax.ShapeDtypeStruct((n, d), BF16),
            jax.ShapeDtypeStruct((n, LANES), F32),
            jax.ShapeDtypeStruct((n, LANES), F32),
            jax.ShapeDtypeStruct((nb, D_HEADS, seq), F32),
            jax.ShapeDtypeStruct((nb, 1, d), F32),
            jax.ShapeDtypeStruct((nb, SUBLANES, d), F32),
        ],
        scratch_shapes=[
            pltpu.VMEM((tm, d), F32), pltpu.VMEM((tm, d), F32), pltpu.VMEM((tm, d), F32),
            pltpu.VMEM((SUBLANES, d), F32), pltpu.VMEM((SUBLANES, d), F32), pltpu.VMEM((SUBLANES, LANES), F32),
        ],
        compiler_params=_cparams("parallel", "arbitrary"),
        name="lru_prompt",
    )(proj, proj, fraw, p["conv_w"], p["conv_b"], p["w_a"], p["b_a"], p["w_x"], p["b_x"], p["lam"], p["b_f"])


def _attn_prompt_kernel(q_ref, k_ref, v_ref, cq_ref, ck_ref, o_ref, m_ref, l_ref, acc_ref, cqb_ref):
    qi = pl.program_id(1)
    ki = pl.program_id(2)
    tq = q_ref.shape[0]
    tk = k_ref.shape[0]
    rep = tk // LANES
    scale2 = (D_HDIM ** -0.5) * LOG2E

    @pl.when(ki == 0)
    def _():
        m_ref[...] = jnp.full_like(m_ref, NEG)
        l_ref[...] = jnp.zeros_like(l_ref)
        acc_ref[...] = jnp.zeros_like(acc_ref)
        cq2 = cq_ref[...] * LOG2E
        for h in range(D_HEADS):
            cqb_ref[h] = jnp.broadcast_to(cq2[:, h:h + 1], (tq, LANES))

    def tile(on_diagonal):
        if on_diagonal:
            causal = (lax.broadcasted_iota(jnp.int32, (tq, tk), 0)
                      >= lax.broadcasted_iota(jnp.int32, (tq, tk), 1))
        ck2 = ck_ref[0] * LOG2E
        for h in range(D_HEADS):
            hs = slice(h * D_HDIM, (h + 1) * D_HDIM)
            t = _dot_nt(q_ref[:, hs], k_ref[:, hs]) * scale2 - ck2[h:h + 1, :]
            if on_diagonal:
                t = jnp.where(causal, t, NEG)
            cqh = cqb_ref[h]
            m_old = m_ref[h]
            rmax = jnp.broadcast_to(jnp.max(t, axis=-1, keepdims=True), (tq, LANES))
            m_new = jnp.maximum(m_old, rmax + cqh)
            alpha = jnp.exp2(m_old - m_new)
            pm = jnp.exp2(t - jnp.concatenate([m_new - cqh] * rep, axis=1))
            rsum = jnp.broadcast_to(jnp.sum(pm, axis=-1, keepdims=True), (tq, LANES))
            l_ref[h] = alpha * l_ref[h] + rsum
            acc_ref[:, hs] = alpha * acc_ref[:, hs] + _dot(pm.astype(BF16), v_ref[:, hs])
            m_ref[h] = m_new

    @pl.when(ki < qi)
    def _():
        tile(False)

    @pl.when(ki == qi)
    def _():
        tile(True)
        for h in range(D_HEADS):
            hs = slice(h * D_HDIM, (h + 1) * D_HDIM)
            o_ref[:, hs] = (acc_ref[:, hs] * (1.0 / l_ref[h])).astype(BF16)


def _attn_prompt(proj, kb, vb, ccol, crow, nb, seq, tq=512):
    n = proj.shape[0]
    d = D_MODEL
    tq = min(tq, seq)
    nq = seq // tq
    return pl.pallas_call(
        _attn_prompt_kernel,
        grid=(nb, nq, nq),
        in_specs=[
            pl.BlockSpec((tq, d), lambda b, qi, ki: (b * nq + qi, 2)),
            pl.BlockSpec((tq, d), lambda b, qi, ki: (b * nq + jnp.minimum(ki, qi), 0)),
            pl.BlockSpec((tq, d), lambda b, qi, ki: (b * nq + jnp.minimum(ki, qi), 0)),
            pl.BlockSpec((tq, LANES), lambda b, qi, ki: (b * nq + qi, 0)),
            pl.BlockSpec((1, D_HEADS, tq), lambda b, qi, ki: (b, 0, jnp.minimum(ki, qi))),
        ],
        out_specs=pl.BlockSpec((tq, d), lambda b, qi, ki: (b * nq + qi, 0)),
        out_shape=jax.ShapeDtypeStruct((n, d), BF16),
        scratch_shapes=[pltpu.VMEM((D_HEADS, tq, LANES), F32), pltpu.VMEM((D_HEADS, tq, LANES), F32),
                        pltpu.VMEM((tq, d), F32), pltpu.VMEM((D_HEADS, tq, LANES), F32)],
        compiler_params=_cparams("parallel", "parallel", "arbitrary"),
        name="attn_prompt",
    )(proj, kb, vb, ccol, crow)


def _odd_sample_kernel(gy_ref, gx_ref, fr_ref, pre_ref, h0_ref, cw_ref, cb_ref, wa_ref, ba_ref, wx_ref, bx_ref,
                       lam_ref, bf_ref, yc_ref, hn_ref, npre_ref, lf_ref):
    xr = gx_ref[...]
    p0, p1, p2 = pre_ref[0], pre_ref[1], pre_ref[2]
    xc = (cb_ref[...] + xr * cw_ref[3:4, :] + p2 * cw_ref[2:3, :] + p1 * cw_ref[1:2, :]
          + p0 * cw_ref[0:1, :])
    npre_ref[0] = p1
    npre_ref[1] = p2
    npre_ref[2] = xr
    a, bt = _lru_gates(xc, wa_ref, ba_ref, wx_ref, bx_ref, lam_ref)
    hn = a * h0_ref[...] + bt
    hn_ref[...] = hn
    yc_ref[...] = (hn * _gelu(gy_ref[...])).astype(BF16)
    lf_ref[...] = _log_sigmoid(fr_ref[...] + bf_ref[...])


def _odd_sample(gy, gx, fraw, prefix, h0, p):
    m, d = gy.shape
    return pl.pallas_call(
        _odd_sample_kernel,
        out_shape=[
            jax.ShapeDtypeStruct((m, d), BF16),
            jax.ShapeDtypeStruct((m, d), F32),
            jax.ShapeDtypeStruct(prefix.shape, F32),
            jax.ShapeDtypeStruct((m, LANES), F32),
        ],
        compiler_params=pltpu.CompilerParams(vmem_limit_bytes=VMEM_LIMIT),
        name="odd_sample",
    )(gy, gx, fraw, prefix, h0, p["conv_w"], p["conv_b"], p["w_a"], p["b_a"], p["w_x"], p["b_x"], p["lam"],
      p["b_f"])


def _page_suffix_sums(x0):
    n = x0.shape[1]
    lane = lax.broadcasted_iota(jnp.int32, x0.shape, 1)
    ex = jnp.where(lane + D_HEADS < n, pltpu.roll(x0, n - D_HEADS, 1), 0.0)
    tot = x0
    s = D_HEADS
    while s < n:
        ex = ex + jnp.where(lane + s < n, pltpu.roll(ex, n - s, 1), 0.0)
        tot = tot + pltpu.roll(tot, n - s, 1)
        s *= 2
    return ex, tot


def _paged_attn_kernel(pt_ref, q_ref, kn_ref, vn_ref, lfn_ref, *rest, pages):
    k_refs = rest[0:pages]
    v_refs = rest[pages:2 * pages]
    lf_refs = rest[2 * pages:3 * pages]
    o_ref, m_ref, l_ref, acc_ref, carry_ref = rest[3 * pages:]
    p = pl.program_id(1)
    npg = pl.num_programs(1)
    scale = D_HDIM ** -0.5
    q = q_ref[0]

    @pl.when(p == 0)
    def _():
        s0 = jnp.sum(q * kn_ref[0], axis=-1, keepdims=True) * scale
        m_ref[...] = jnp.broadcast_to(s0, m_ref.shape)
        l_ref[...] = jnp.ones_like(l_ref)
        acc_ref[...] = vn_ref[0]
        carry_ref[0:1, :] = lfn_ref[0]

    nrow = k_refs[0].shape[1]
    qb = q.astype(BF16)
    lane = lax.broadcasted_iota(jnp.int32, (D_HEADS, nrow), 1)
    sub = lax.broadcasted_iota(jnp.int32, (D_HEADS, nrow), 0)
    own = (lane & (D_HEADS - 1)) == sub
    carry = carry_ref[0:1, :]
    ex, tot = _page_suffix_sums(jnp.concatenate([lf_refs[j][0] for j in range(pages)], axis=0))
    scores = []
    for j in range(pages):
        s = _dot_nt(qb, k_refs[j][0].astype(BF16)) * scale
        scores.append(jnp.where(own, s + (ex[j:j + 1, :] + carry), NEG))
        carry = carry + tot[j:j + 1, :]
    carry_ref[0:1, :] = carry
    mx = jnp.max(scores[0], axis=-1, keepdims=True)
    for s in scores[1:]:
        mx = jnp.maximum(mx, jnp.max(s, axis=-1, keepdims=True))
    m_old = m_ref[:, 0:1]
    m_new = jnp.maximum(m_old, mx)
    alpha = jnp.exp(m_old - m_new)
    lsum = None
    pv = None
    for j in range(pages):
        pm = jnp.exp(scores[j] - m_new)
        ls = jnp.sum(pm, axis=-1, keepdims=True)
        r = _dot(pm.astype(BF16), v_refs[j][0].astype(BF16))
        lsum = ls if lsum is None else lsum + ls
        pv = r if pv is None else pv + r
    l_ref[...] = alpha * l_ref[...] + lsum
    acc_ref[...] = alpha * acc_ref[...] + pv
    m_ref[...] = jnp.broadcast_to(m_new, m_ref.shape)

    @pl.when(p == npg - 1)
    def _():
        o_ref[0] = acc_ref[...] / l_ref[...]


def _paged_attn(page_table, q, kn, vn, lfn_rep, kc, vc, lfc, page_base, pages=4):
    m, npg = page_table.shape
    nrow = kc.shape[1]
    pages = math.gcd(npg, pages)

    def pg(j):
        return lambda b, p, pt: (page_base + pt[b, npg - 1 - (p * pages + j)], 0, 0)

    tok = pl.BlockSpec((1, D_HEADS, D_HDIM), lambda b, p, pt: (b, 0, 0))
    flat = pl.BlockSpec((1, 1, nrow), lambda b, p, pt: (b, 0, 0))
    big = [pl.BlockSpec((1, nrow, D_HDIM), pg(j)) for j in range(pages)]
    small = [pl.BlockSpec((1, 1, nrow), pg(j)) for j in range(pages)]
    grid_spec = pltpu.PrefetchScalarGridSpec(
        num_scalar_prefetch=1,
        grid=(m, npg // pages),
        in_specs=[tok, tok, tok, flat] + big + big + small,
        out_specs=tok,
        scratch_shapes=[
            pltpu.VMEM((D_HEADS, D_HDIM), F32), pltpu.VMEM((D_HEADS, D_HDIM), F32),
            pltpu.VMEM((D_HEADS, D_HDIM), F32), pltpu.VMEM((SUBLANES, nrow), F32),
        ],
    )
    return pl.pallas_call(
        functools.partial(_paged_attn_kernel, pages=pages),
        grid_spec=grid_spec,
        out_shape=jax.ShapeDtypeStruct((m, D_HEADS, D_HDIM), F32),
        compiler_params=_cparams("parallel", "arbitrary"),
        name="paged_attn",
    )(page_table, q, kn, vn, lfn_rep, *([kc] * pages), *([vc] * pages), *([lfc] * pages))


def _pad_lanes(v, n=LANES):
    v = v.reshape(1, -1).astype(F32)
    return jnp.pad(v, ((0, 0), (0, n - v.shape[1])))


def _pad_cols(w, n=LANES):
    return jnp.pad(w, ((0, 0), (0, n - w.shape[1])))


def _row(v):
    return v.reshape(1, -1).astype(F32)


def _expand_matrix():
    e = np.zeros((LANES, D_MODEL), np.float32)
    for h in range(B_HEADS):
        e[h, h * B_HDIM:(h + 1) * B_HDIM] = 1.0
    return jnp.asarray(e, BF16)


def kernel(x_prompt, x_sample, state_ssm, state_ssm_conv, state_lru, state_lru_conv, cache_k, cache_v, cache_logf, state_ffn_conv, page_table, norm_mix, norm_ffn, norm_final, w_in_even, sgu_ln_g, sgu_ln_b, sgu_w, sgu_b, ssd_conv_w, ssd_conv_b, ssd_dt_bias, ssd_a_log, ssd_d, ssd_norm_g, w_out_even, w_in_odd, lru_conv_w, lru_conv_b, lru_w_a, lru_b_a, lru_w_x, lru_b_x, lru_lambda, fox_b_f, w_out_odd, ffn_w_g, ffn_w_u, ffn_conv_w, ffn_conv_b, ffn_w_d):
    nb, seq, d = x_prompt.shape
    ms = x_sample.shape[0]
    depth = norm_mix.shape[0]
    n_pool = cache_k.shape[1]
    xp = x_prompt.reshape(nb * seq, d)
    xs = x_sample.reshape(ms, d)
    gs = B_GROUPS * B_STATE
    main_even = 3 * d + B_CONV_DIM

    res = {k: [] for k in ("sgu_v", "ssm_p", "ssm_s", "sconv_p", "sconv_s", "lru_p", "lru_s", "lconv_p",
                           "lconv_s", "k_p", "v_p", "f_p", "k_s", "v_s", "f_s", "fconv_p", "fconv_s")}

    wg_all = ffn_w_g.astype(BF16)
    wu_all = ffn_w_u.astype(BF16)
    wd_all = ffn_w_d.astype(BF16)

    for l in range(depth):
        li = l // 2
        if l % 2 == 0:
            w_in = w_in_even[li]
            w_bf = w_in.astype(BF16)
            w_dt = _pad_cols(w_in[:, main_even:]).astype(BF16)
            p = {
                "ln_g": _row(sgu_ln_g[li]), "ln_b": _row(sgu_ln_b[li]),
                "sgu_w": sgu_w[li].astype(F32), "sgu_bt": sgu_b[li].astype(F32).T,
                "sgu_w0": _row(jnp.repeat(sgu_w[li][:, 0, 0], A_GDIM)),
                "sgu_b0": _row(jnp.repeat(sgu_b[li][:, 0], A_GDIM)),
                "conv_w": ssd_conv_w[li].astype(F32), "conv_b": _row(ssd_conv_b[li]),
                "dt_bias": _pad_lanes(ssd_dt_bias[li]), "a_log": _pad_lanes(ssd_a_log[li]),
                "a_log_x": _row(jnp.repeat(ssd_a_log[li], B_HDIM)),
                "d_skip": _row(jnp.repeat(ssd_d[li], B_HDIM)), "norm_g": _row(ssd_norm_g[li]),
                "expand": _expand_matrix(),
            }
            w_out = w_out_even[li].astype(BF16)

            proj, dtraw = _rms_matmul(xp, norm_mix[l], w_bf, (BF16,), 0, main_even, w_side=w_dt, tn=1536)
            ya, yb, tails, state = _even_prompt(proj, dtraw, nb, seq, p)
            xp = _matmul_res(ya, yb, w_out, xp)
            res["ssm_p"].append(state.reshape(nb, B_HEADS, B_HDIM, B_STATE))
            res["sconv_p"].append(tails[:, SUBLANES - 3:, :])

            proj_s, dtraw_s = _rms_matmul(xs, norm_mix[l], w_bf, (F32,), 0, main_even, w_side=w_dt)
            prefix = jnp.swapaxes(state_ssm_conv[li], 0, 1)
            ya_s, v_s, npre, xs_s, bm_s, cm_s, dtx_s, dec_s = _even_sample_a(proj_s, dtraw_s, prefix, p)
            hnew, yrow = _ssm_state_sample(
                state_ssm[li].reshape(ms, d, B_STATE), dtx_s.reshape(ms, 1, d), dec_s.reshape(ms, 1, d),
                bm_s.reshape(ms, B_GROUPS, B_STATE), cm_s.reshape(ms, B_GROUPS, B_STATE))
            yb_s = _even_sample_c(yrow.reshape(ms, d), xs_s, proj_s[:, 2 * d:3 * d], p)
            xs = _matmul_res(ya_s, yb_s, w_out, xs)
            res["sgu_v"].append(v_s.reshape(ms, 1, d))
            res["ssm_s"].append(hnew.reshape(ms, B_HEADS, B_HDIM, B_STATE))
            res["sconv_s"].append(jnp.swapaxes(npre, 0, 1))
        else:
            w_in = w_in_odd[li]
            w_bf = w_in.astype(BF16)
            w_f = _pad_cols(w_in[:, 5 * d:]).astype(BF16)
            p = {
                "conv_w": lru_conv_w[li].astype(F32), "conv_b": _row(lru_conv_b[li]),
                "w_a": lru_w_a[li].astype(BF16), "b_a": _row(lru_b_a[li]),
                "w_x": lru_w_x[li].astype(BF16), "b_x": _row(lru_b_x[li]),
                "lam": _row(lru_lambda[li]), "b_f": _pad_lanes(fox_b_f[li]),
            }
            w_out = w_out_odd[li].astype(BF16)

            proj, fraw = _rms_matmul(xp, norm_mix[l], w_bf, (BF16,), 0, 3 * d, w_side=w_f, tn=2048)
            kf, kb, vf, vb = _kv_proj(xp, norm_mix[l], w_bf, 3 * d, 4 * d)
            yc, lf, ccol, crow, last, ctail = _lru_prompt(proj, fraw, nb, seq, p)
            yd = _attn_prompt(proj, kb, vb, ccol, crow, nb, seq)
            xp = _matmul_res(yc, yd, w_out, xp)
            res["lru_p"].append(last.reshape(nb, d))
            res["lconv_p"].append(ctail[:, SUBLANES - 3:, :])
            res["k_p"].append(kf.reshape(nb, seq, D_HEADS, D_HDIM))
            res["v_p"].append(vf.reshape(nb, seq, D_HEADS, D_HDIM))
            res["f_p"].append(lf[:, :D_HEADS].reshape(nb, seq, D_HEADS))

            proj_s, fraw_s = _rms_matmul(xs, norm_mix[l], w_bf, (F32,), 0, 5 * d, w_side=w_f)
            prefix = jnp.swapaxes(state_lru_conv[li], 0, 1)
            yc_s, hn_s, npre, lf_s = _odd_sample(proj_s[:, :d], proj_s[:, d:2 * d], fraw_s, prefix,
                                                 state_lru[li].astype(F32), p)
            q_s = proj_s[:, 2 * d:3 * d].reshape(ms, D_HEADS, D_HDIM)
            k_s = proj_s[:, 3 * d:4 * d].reshape(ms, D_HEADS, D_HDIM)
            v_s = proj_s[:, 4 * d:5 * d].reshape(ms, D_HEADS, D_HDIM)
            lfn = lf_s[:, :D_HEADS]
            nrow = PAGE * D_HEADS
            yd_s = _paged_attn(
                page_table, q_s, k_s, v_s, jnp.tile(lfn, (1, PAGE)).reshape(ms, 1, nrow),
                cache_k.reshape(-1, nrow, D_HDIM), cache_v.reshape(-1, nrow, D_HDIM),
                cache_logf.reshape(-1, 1, nrow).astype(F32), li * n_pool)
            xs = _matmul_res(yc_s, yd_s.reshape(ms, d).astype(BF16), w_out, xs)
            res["lru_s"].append(hn_s)
            res["lconv_s"].append(jnp.swapaxes(npre, 0, 1))
            res["k_s"].append(k_s.reshape(ms, 1, D_HEADS, D_HDIM))
            res["v_s"].append(v_s.reshape(ms, 1, D_HEADS, D_HDIM))
            res["f_s"].append(lfn.reshape(ms, 1, D_HEADS))

        last_layer = l == depth - 1
        cw = ffn_conv_w[l].astype(F32)
        xp, ftails = _ffn_prompt(xp, norm_ffn[l], wg_all, wu_all, cw, ffn_conv_b[l], wd_all, norm_final, l, seq,
                                 last_layer)
        xs, fpre = _ffn_sample(xs, norm_ffn[l], wg_all, wu_all, cw, ffn_conv_b[l],
                               jnp.swapaxes(state_ffn_conv[l], 0, 1).astype(F32), wd_all, norm_final, l,
                               last_layer)
        res["fconv_p"].append(ftails[:, SUBLANES - 2:, :])
        res["fconv_s"].append(jnp.swapaxes(fpre, 0, 1))

    st = lambda k: jnp.stack(res[k])
    return (xp.reshape(nb, seq, d), xs.reshape(ms, 1, d), st("sgu_v"), st("ssm_p"), st("ssm_s"), st("sconv_p"),
            st("sconv_s"), st("lru_p"), st("lru_s"), st("lconv_p"), st("lconv_s"), st("k_p"), st("v_p"),
            st("f_p"), st("k_s"), st("v_s"), st("f_s"), st("fconv_p"), st("fconv_s"))
```
